```python
import math
import jax
import jax.numpy as jnp
from jax import lax
import numpy as np

D_MODEL = 1024
BATCH = 4
SEQ = 8192
DEPTH = 2
DEC_BATCH = 32
DEC_SEQ = 8
PAST_LEN = 16384
PAGE_SIZE = 128

N_A_LAYERS = DEPTH // 2
N_B_LAYERS = DEPTH - N_A_LAYERS
SSM_EXPAND = 2
D_INNER = SSM_EXPAND * D_MODEL
SSM_HEAD_DIM = 64
N_SSM_HEADS = D_INNER // SSM_HEAD_DIM
D_STATE = 128
N_GROUPS = 8
HEADS_PER_GROUP = N_SSM_HEADS // N_GROUPS
CONV_W = 4
CONV_DIM = D_INNER + 2 * N_GROUPS * D_STATE
D_IN_PROJ = D_INNER + CONV_DIM + N_SSM_HEADS
SSD_CHUNK = 128
DT_MIN = 1e-3
DT_MAX = 1e-1
HEAD_DIM = 64
N_HEADS = D_MODEL // HEAD_DIM
SB_BLOCK = 128
SB_BIAS_INIT = -6.0
N_EXPERTS = 32
TOP_K = 4
D_EXPERT = D_MODEL
SWIGLU_ALPHA = 1.702
SWIGLU_LIMIT = 7.0
MOE_BLOCK = 128
RMS_EPS = 1e-5

kernel_name = "yoco_mamba2_stickbreaking_moe_step"


def rmsnorm(x, g):
    xf = x.astype(jnp.float32)
    xf = xf * lax.rsqrt(jnp.mean(xf * xf, axis=-1, keepdims=True) + RMS_EPS)
    return (xf * g.astype(jnp.float32)).astype(x.dtype)


def gated_rmsnorm(y, z, g):
    yf = y.astype(jnp.float32) * jax.nn.silu(z.astype(jnp.float32))
    yg = yf.reshape(*yf.shape[:-1], N_GROUPS, D_INNER // N_GROUPS)
    yg = yg * lax.rsqrt(jnp.mean(yg * yg, axis=-1, keepdims=True) + RMS_EPS)
    return (yg.reshape(yf.shape) * g.astype(jnp.float32)).astype(z.dtype)


def ssd_scan(x, dt, a_head, bm, cm, init_state):
    b, l, h, p = x.shape
    q = math.gcd(l, SSD_CHUNK)
    c = l // q
    xg = (x.astype(jnp.float32) * dt[..., None]).reshape(b, c, q, N_GROUPS, HEADS_PER_GROUP, p)
    a = (dt * a_head).reshape(b, c, q, N_GROUPS, HEADS_PER_GROUP)
    bc = bm.astype(jnp.float32).reshape(b, c, q, N_GROUPS, D_STATE)
    cc = cm.astype(jnp.float32).reshape(b, c, q, N_GROUPS, D_STATE)
    a_cum = jnp.cumsum(a, axis=2)
    seg = a_cum[:, :, :, None] - a_cum[:, :, None, :]
    tril = jnp.tril(jnp.ones((q, q), dtype=bool))[None, None, :, :, None, None]
    decay = jnp.exp(jnp.where(tril, seg, -jnp.inf))
    cb = jnp.einsum('bcign,bcjgn->bcijg', cc, bc)
    y_diag = jnp.einsum('bcijg,bcijge,bcjgep->bcigep', cb, decay, xg)
    decay_to_end = jnp.exp(a_cum[:, :, -1:] - a_cum)
    chunk_states = jnp.einsum('bcjgn,bcjge,bcjgep->bcgepn', bc, decay_to_end, xg)
    chunk_decay = jnp.exp(a_cum[:, :, -1])

    def step(s, inp):
        st, dec = inp
        return s * dec[..., None, None] + st, s

    s0 = init_state.astype(jnp.float32).reshape(b, N_GROUPS, HEADS_PER_GROUP, p, D_STATE)
    final, starts = lax.scan(step, s0, (jnp.moveaxis(chunk_states, 1, 0), jnp.moveaxis(chunk_decay, 1, 0)))
    starts = jnp.moveaxis(starts, 0, 1)
    y_off = jnp.einsum('bcign,bcige,bcgepn->bcigep', cc, jnp.exp(a_cum), starts)
    y = (y_diag + y_off).reshape(b, l, h, p)
    return y, final.reshape(b, h, p, D_STATE)


def mamba2_mixer(h, conv_state, ssm_state, w_in, conv_w, conv_b, dt_bias, a_log, d_skip, g_norm, w_out):
    b, l, _ = h.shape
    zxbcdt = h @ w_in
    z = zxbcdt[..., :D_INNER]
    xbc = zxbcdt[..., D_INNER:D_INNER + CONV_DIM]
    dt = zxbcdt[..., D_INNER + CONV_DIM:]
    xbc_ext = jnp.concatenate([conv_state.astype(xbc.dtype), xbc], axis=1)
    new_conv = xbc_ext[:, -(CONV_W - 1):]
    acc = conv_b.astype(xbc.dtype)
    for w in range(CONV_W):
        acc = acc + xbc_ext[:, w:w + l] * conv_w[w]
    xbc = jax.nn.silu(acc)
    xs = xbc[..., :D_INNER].reshape(b, l, N_SSM_HEADS, SSM_HEAD_DIM)
    bm = xbc[..., D_INNER:D_INNER + N_GROUPS * D_STATE].reshape(b, l, N_GROUPS, D_STATE)
    cm = xbc[..., D_INNER + N_GROUPS * D_STATE:].reshape(b, l, N_GROUPS, D_STATE)
    dt = jax.nn.softplus(dt.astype(jnp.float32) + dt_bias.astype(jnp.float32))
    a_head = -jnp.exp(a_log.astype(jnp.float32))
    y, new_ssm = ssd_scan(xs, dt, a_head, bm, cm, ssm_state)
    y = y + xs.astype(jnp.float32) * d_skip.astype(jnp.float32)[:, None]
    y = gated_rmsnorm(y.reshape(b, l, D_INNER), z, g_norm)
    return y @ w_out, new_conv, new_ssm


def stick_breaking_block(q, q_pos, k, v, k_pos, bias):
    z = jnp.einsum('bqhd,bshd->bhqs', q.astype(jnp.float32), k.astype(jnp.float32)) * (HEAD_DIM ** -0.5)
    z = z + bias.astype(jnp.float32)[None, :, None, None]
    mask = (k_pos[None, :] < q_pos[:, None])[None, None]
    log_keep = jnp.where(mask, jax.nn.log_sigmoid(-z), 0.0)
    log_w = jax.nn.log_sigmoid(z) + lax.cumsum(log_keep, axis=3, reverse=True) - log_keep
    w = jnp.where(mask, jnp.exp(log_w), 0.0)
    return jnp.einsum('bhqs,bshd->bqhd', w, v.astype(jnp.float32)).astype(q.dtype)


def sb_attend_prompt(q, k, v, bias):
    b, l, h, d = q.shape
    nb = l // SB_BLOCK
    pos = jnp.arange(l)
    qb = jnp.moveaxis(q.reshape(b, nb, SB_BLOCK, h, d), 1, 0)
    pb = pos.reshape(nb, SB_BLOCK)
    ob = lax.map(lambda a: stick_breaking_block(a[0], a[1], k, v, pos, bias), (qb, pb))
    return jnp.moveaxis(ob, 0, 1).reshape(b, l, h, d)


def sb_attend_sample(q, k_new, v_new, bias, cache_k, cache_v, page_table):
    n_pages = page_table.shape[1]
    past = n_pages * PAGE_SIZE
    l = q.shape[1]
    q_pos = past + jnp.arange(l)
    k_pos = jnp.arange(past + l)

    def one(args):
        q1, kn, vn, pages = args
        kp = jnp.take(cache_k, pages, axis=0).reshape(past, N_HEADS, HEAD_DIM).astype(kn.dtype)
        vp = jnp.take(cache_v, pages, axis=0).reshape(past, N_HEADS, HEAD_DIM).astype(vn.dtype)
        k1 = jnp.concatenate([kp, kn], axis=0)
        v1 = jnp.concatenate([vp, vn], axis=0)
        return stick_breaking_block(q1[None], q_pos, k1[None], v1[None], k_pos, bias)[0]

    return lax.map(one, (q, k_new, v_new, page_table))


def clamped_swiglu(gu):
    x_glu = jnp.minimum(gu[..., :D_EXPERT], SWIGLU_LIMIT)
    x_lin = jnp.clip(gu[..., D_EXPERT:], -SWIGLU_LIMIT, SWIGLU_LIMIT)
    return (x_lin + 1.0) * (x_glu * jax.nn.sigmoid(SWIGLU_ALPHA * x_glu))


def moe_ffn(h, w_router, b_router, w_gu, b_gu, w_down, b_down):
    shp = h.shape
    x = h.reshape(-1, D_MODEL)
    t = x.shape[0]
    n_assign = t * TOP_K
    logits = x.astype(jnp.float32) @ w_router.astype(jnp.float32) + b_router.astype(jnp.float32)
    top_val, top_idx = lax.top_k(logits, TOP_K)
    gates = jax.nn.softmax(top_val, axis=-1).astype(x.dtype)
    flat_e = top_idx.reshape(n_assign).astype(jnp.int32)
    order = jnp.argsort(flat_e)
    sorted_e = flat_e[order]
    counts = jnp.zeros((N_EXPERTS,), jnp.int32).at[flat_e].add(1)
    padded = (counts + MOE_BLOCK - 1) // MOE_BLOCK * MOE_BLOCK
    pad_end = jnp.cumsum(padded)
    pad_start = pad_end - padded
    start = jnp.cumsum(counts) - counts
    dest = pad_start[sorted_e] + jnp.arange(n_assign, dtype=jnp.int32) - start[sorted_e]
    n_blocks = -(-(n_assign + N_EXPERTS * (MOE_BLOCK - 1)) // MOE_BLOCK)
    n_slots = n_blocks * MOE_BLOCK
    slot_token = jnp.full((n_slots,), t, jnp.int32).at[dest].set((order // TOP_K).astype(jnp.int32))
    block_expert = jnp.minimum(
        jnp.searchsorted(pad_end, jnp.arange(n_blocks, dtype=jnp.int32) * MOE_BLOCK, side='right'),
        N_EXPERTS - 1)
    x_pad = jnp.concatenate([x, jnp.zeros((1, D_MODEL), x.dtype)], axis=0)

    def run_block(args):
        tok, e = args
        xb = x_pad[tok]
        act = clamped_swiglu(xb @ w_gu[e] + b_gu[e])
        return act @ w_down[e] + b_down[e]

    y_slots = lax.map(run_block, (slot_token.reshape(n_blocks, MOE_BLOCK), block_expert))
    y_slots = y_slots.reshape(n_slots, D_MODEL)
    dest_orig = jnp.zeros((n_assign,), jnp.int32).at[order].set(dest)
    y = jnp.einsum('tk,tkd->td', gates, y_slots[dest_orig].reshape(t, TOP_K, D_MODEL))
    return y.reshape(shp)


def _trunk(x, conv_init, ssm_init, attend, p):
    b, l, _ = x.shape
    h = x
    new_conv, new_ssm = [], []
    k = v = None
    for layer in range(DEPTH):
        if layer < N_A_LAYERS:
            i = layer
            mix, cs, ss = mamba2_mixer(rmsnorm(h, p['g_mix'][i]), conv_init[i], ssm_init[i],
                                       p['w_in'][i], p['conv_w'][i], p['conv_b'][i], p['dt_bias'][i],
                                       p['a_log'][i], p['d_skip'][i], p['g_ssm_norm'][i], p['w_ssm_out'][i])
            new_conv.append(cs)
            new_ssm.append(ss)
        else:
            j = layer - N_A_LAYERS
            if j == 0:
                hkv = rmsnorm(h, p['g_kv'])
                k = (hkv @ p['w_k']).reshape(b, l, N_HEADS, HEAD_DIM)
                v = (hkv @ p['w_v']).reshape(b, l, N_HEADS, HEAD_DIM)
            q = (rmsnorm(h, p['g_attn'][j]) @ p['w_q'][j]).reshape(b, l, N_HEADS, HEAD_DIM)
            mix = attend(q, k, v, p['sb_bias'][j]).reshape(b, l, N_HEADS * HEAD_DIM) @ p['w_o'][j]
        h = h + mix
        h = h + moe_ffn(rmsnorm(h, p['g_ffn'][layer]), p['w_router'][layer], p['b_router'][layer],
                        p['w_gu'][layer], p['b_gu'][layer], p['w_down'][layer], p['b_down'][layer])
    return rmsnorm(h, p['g_final']), jnp.stack(new_conv), jnp.stack(new_ssm), k, v


def setup_inputs(seed: int = 0) -> dict:
    key = jax.random.key(seed)
    ks = jax.random.split(key, 40)
    f32 = jnp.float32
    n_pages = PAST_LEN // PAGE_SIZE
    n_phys = (DEC_BATCH * n_pages * 5) // 4
    nrm = lambda k, shape, s: jax.random.normal(k, shape, f32) * s
    gain = lambda k, shape: 1.0 + 0.01 * jax.random.normal(k, shape, f32)
    dt0 = jnp.exp(jax.random.uniform(ks[9], (N_A_LAYERS, N_SSM_HEADS), f32)
                  * (math.log(DT_MAX) - math.log(DT_MIN)) + math.log(DT_MIN))
    page_table = jax.random.permutation(ks[6], n_phys)[:DEC_BATCH * n_pages].reshape(DEC_BATCH, n_pages).astype(jnp.int32)
    return {
        'x_prompt': nrm(ks[0], (BATCH, SEQ, D_MODEL), 1.0),
        'x_sample': nrm(ks[1], (DEC_BATCH, DEC_SEQ, D_MODEL), 1.0),
        'state_conv': nrm(ks[2], (N_A_LAYERS, DEC_BATCH, CONV_W - 1, CONV_DIM), 1.0),
        'state_ssm': nrm(ks[3], (N_A_LAYERS, DEC_BATCH, N_SSM_HEADS, SSM_HEAD_DIM, D_STATE), 0.1),
        'cache_k': nrm(ks[4], (n_phys, PAGE_SIZE, N_HEADS, HEAD_DIM), 1.0),
        'cache_v': nrm(ks[5], (n_phys, PAGE_SIZE, N_HEADS, HEAD_DIM), 1.0),
        'page_table': page_table,
        'g_mix': gain(ks[7], (N_A_LAYERS, D_MODEL)),
        'w_in': nrm(ks[8], (N_A_LAYERS, D_MODEL, D_IN_PROJ), D_MODEL ** -0.5),
        'conv_w': nrm(ks[10], (N_A_LAYERS, CONV_W, CONV_DIM), CONV_W ** -0.5),
        'conv_b': nrm(ks[11], (N_A_LAYERS, CONV_DIM), 0.01),
        'dt_bias': dt0 + jnp.log(-jnp.expm1(-dt0)),
        'a_log': jnp.log(jax.random.uniform(ks[12], (N_A_LAYERS, N_SSM_HEADS), f32, 1.0, 16.0)),
        'd_skip': gain(ks[13], (N_A_LAYERS, N_SSM_HEADS)),
        'g_ssm_norm': gain(ks[14], (N_A_LAYERS, D_INNER)),
        'w_ssm_out': nrm(ks[15], (N_A_LAYERS, D_INNER, D_MODEL), D_INNER ** -0.5),
        'g_kv': gain(ks[16], (D_MODEL,)),
        'w_k': nrm(ks[17], (D_MODEL, N_HEADS * HEAD_DIM), D_MODEL ** -0.5),
        'w_v': nrm(ks[18], (D_MODEL, N_HEADS * HEAD_DIM), D_MODEL ** -0.5),
        'g_attn': gain(ks[19], (N_B_LAYERS, D_MODEL)),
        'w_q': nrm(ks[20], (N_B_LAYERS, D_MODEL, N_HEADS * HEAD_DIM), D_MODEL ** -0.5),
        'w_o': nrm(ks[21], (N_B_LAYERS, N_HEADS * HEAD_DIM, D_MODEL), (N_HEADS * HEAD_DIM) ** -0.5),
        'sb_bias': SB_BIAS_INIT + 0.1 * jax.random.normal(ks[30], (N_B_LAYERS, N_HEADS), f32),
        'g_ffn': gain(ks[22], (DEPTH, D_MODEL)),
        'w_router': nrm(ks[23], (DEPTH, D_MODEL, N_EXPERTS), D_MODEL ** -0.5),
        'b_router': nrm(ks[24], (DEPTH, N_EXPERTS), 0.01),
        'w_gu': nrm(ks[25], (DEPTH, N_EXPERTS, D_MODEL, 2 * D_EXPERT), D_MODEL ** -0.5),
        'b_gu': nrm(ks[26], (DEPTH, N_EXPERTS, 2 * D_EXPERT), 0.01),
        'w_down': nrm(ks[27], (DEPTH, N_EXPERTS, D_EXPERT, D_MODEL), D_EXPERT ** -0.5),
        'b_down': nrm(ks[28], (DEPTH, N_EXPERTS, D_MODEL), 0.01),
        'g_final': gain(ks[29], (D_MODEL,)),
    }


def reference(x_prompt, x_sample, state_conv, state_ssm, cache_k, cache_v, page_table,
              g_mix, w_in, conv_w, conv_b, dt_bias, a_log, d_skip, g_ssm_norm, w_ssm_out,
              g_kv, w_k, w_v, g_attn, w_q, w_o, sb_bias,
              g_ffn, w_router, b_router, w_gu, b_gu, w_down, b_down, g_final):
    p = {'g_mix': g_mix, 'w_in': w_in, 'conv_w': conv_w, 'conv_b': conv_b, 'dt_bias': dt_bias,
         'a_log': a_log, 'd_skip': d_skip, 'g_ssm_norm': g_ssm_norm, 'w_ssm_out': w_ssm_out,
         'g_kv': g_kv, 'w_k': w_k, 'w_v': w_v, 'g_attn': g_attn, 'w_q': w_q, 'w_o': w_o,
         'sb_bias': sb_bias,
         'g_ffn': g_ffn, 'w_router': w_router, 'b_router': b_router, 'w_gu': w_gu, 'b_gu': b_gu,
         'w_down': w_down, 'b_down': b_down, 'g_final': g_final}
    b_p = x_prompt.shape[0]
    conv0 = jnp.zeros((N_A_LAYERS, b_p, CONV_W - 1, CONV_DIM), x_prompt.dtype)
    ssm0 = jnp.zeros((N_A_LAYERS, b_p, N_SSM_HEADS, SSM_HEAD_DIM, D_STATE), jnp.float32)
    y_prompt, p_conv, p_ssm, p_k, p_v = _trunk(x_prompt, conv0, ssm0, sb_attend_prompt, p)
    attend_sample = lambda q, k, v, bias: sb_attend_sample(q, k, v, bias, cache_k, cache_v, page_table)
    y_sample, s_conv, s_ssm, s_k, s_v = _trunk(x_sample, state_conv, state_ssm, attend_sample, p)
    return (y_prompt, y_sample, p_conv, p_ssm, p_k, p_v, s_conv, s_ssm, s_k, s_v)
```

```python
import functools
import math

import jax
import jax.numpy as jnp
from jax import lax
from jax.experimental import pallas as pl
from jax.experimental.pallas import tpu as pltpu

F32 = jnp.float32
BF16 = jnp.bfloat16
HIGHEST = lax.Precision.HIGHEST

LANES = 128
SUBLANES = 8
VMEM_LIMIT = 48 * 1024 * 1024

D_MODEL = 1024
D_INNER = 2048
SSM_HEAD_DIM = 64
N_SSM_HEADS = 32
D_STATE = 128
N_GROUPS = 8
GROUP_W = D_INNER // N_GROUPS
CONV_W = 4
CONV_DIM = D_INNER + 2 * N_GROUPS * D_STATE
SSD_CHUNK = 128
HEAD_DIM = 64
N_HEADS = 16
N_EXPERTS = 32
TOP_K = 4
D_EXPERT = 1024
SWIGLU_ALPHA = 1.702
SWIGLU_LIMIT = 7.0
RMS_EPS = 1e-5
PAGE_SIZE = 128

MOE_BM = 256
ATTN_TQ = 256
ATTN_TK = 256
PAGES_PER_STEP = 4
NEG_BIG = -1e30


def _cparams(*sem):
    return pltpu.CompilerParams(dimension_semantics=sem, vmem_limit_bytes=VMEM_LIMIT)


def _softplus(z):
    return jnp.maximum(z, 0.0) + jnp.log1p(jnp.exp(-jnp.abs(z)))


def _split_bf16(x):
    hi = x.astype(BF16)
    lo = (x - hi.astype(F32)).astype(BF16)
    return hi, lo


def _nm_body(*refs, norm, has_res, n_out, precision):
    x_ref, w_ref = refs[0], refs[1]
    pos = 2
    res_ref = None
    if has_res:
        res_ref = refs[pos]
        pos += 1
    out_refs = refs[pos:pos + n_out]
    pos += n_out
    if norm:
        xn_ref = refs[pos]

        @pl.when(pl.program_id(1) == 0)
        def _():
            x = x_ref[...].astype(F32)
            ms = jnp.mean(x * x, axis=-1, keepdims=True)
            xn_ref[...] = (x * lax.rsqrt(ms + RMS_EPS)).astype(xn_ref.dtype)

        lhs = xn_ref[...]
    else:
        lhs = x_ref[...]
    acc = jnp.dot(lhs, w_ref[...], preferred_element_type=F32, precision=precision)
    if has_res:
        acc = acc + res_ref[...]
    for o in out_refs:
        o[...] = acc.astype(o.dtype)


def _norm_matmul(x, w, *, norm, res=None, out_dtypes=(F32,), tm, tn, precision=None, name="matmul"):
    t, k = x.shape
    n = w.shape[1]
    assert t % tm == 0 and n % tn == 0
    in_specs = [pl.BlockSpec((tm, k), lambda i, j: (i, 0)),
                pl.BlockSpec((k, tn), lambda i, j: (0, j))]
    args = [x, w]
    if res is not None:
        in_specs.append(pl.BlockSpec((tm, tn), lambda i, j: (i, j)))
        args.append(res)
    outs = pl.pallas_call(
        functools.partial(_nm_body, norm=norm, has_res=res is not None,
                          n_out=len(out_dtypes), precision=precision),
        grid=(t // tm, n // tn),
        in_specs=in_specs,
        out_specs=[pl.BlockSpec((tm, tn), lambda i, j: (i, j)) for _ in out_dtypes],
        out_shape=[jax.ShapeDtypeStruct((t, n), d) for d in out_dtypes],
        scratch_shapes=[pltpu.VMEM((tm, k), w.dtype)] if norm else [],
        compiler_params=_cparams("parallel", "arbitrary"),
        name=name,
    )(*args)
    return outs


def _ssd_body(zx_ref, dt_ref, conv0_ref, ssm0_ref, cw_ref, cb_ref, dtb_ref, alog_ref,
              dskip_ref, gn_ref, e_ref,
              y_ref, conv_out_ref, ssm_ref,
              ext_ref, xbc_ref, xg_ref, xd_ref, *, valid_len):
    L = SSD_CHUNK
    c = pl.program_id(1)

    @pl.when(c == 0)
    def _():
        ext_ref[0:SUBLANES, :] = conv0_ref[0]
        ssm_ref[0] = ssm0_ref[0]

    ext_ref[SUBLANES:SUBLANES + L, :] = zx_ref[:, D_INNER:D_INNER + CONV_DIM]
    strip = 512
    for s in range(CONV_DIM // strip):
        cols = slice(s * strip, (s + 1) * strip)
        acc = jnp.broadcast_to(cb_ref[:, cols], (L, strip))
        for w in range(CONV_W):
            r0 = SUBLANES - (CONV_W - 1) + w
            acc = acc + ext_ref[r0:r0 + L, cols] * cw_ref[w:w + 1, cols]
        xbc_ref[:, cols] = acc * jax.nn.sigmoid(acc)
    conv_out_ref[0] = ext_ref[valid_len:valid_len + SUBLANES, :]
    ext_ref[0:SUBLANES, :] = ext_ref[L:L + SUBLANES, :]

    lane = lax.broadcasted_iota(jnp.int32, (L, LANES), 1)
    row = lax.broadcasted_iota(jnp.int32, (L, LANES), 0)
    live = (lane < N_SSM_HEADS) & (row < valid_len)
    dt = jnp.where(live, _softplus(dt_ref[...] + dtb_ref[...]), 0.0)
    a = dt * (-jnp.exp(alog_ref[...]))
    tril = (lax.broadcasted_iota(jnp.int32, (L, L), 0)
            >= lax.broadcasted_iota(jnp.int32, (L, L), 1))
    a_cum = jnp.dot(tril.astype(F32), a, preferred_element_type=F32, precision=HIGHEST)
    a_cum_t = a_cum.T
    a_last = a_cum[L - 1:L, :]
    ea = jnp.exp(a_cum)
    w_end = dt * jnp.exp(a_last - a_cum)
    chunk_decay_t = jnp.exp(a_cum_t[:, L - 1:L])

    def expand(p):
        hi, lo = _split_bf16(p)
        e = e_ref[...]
        return (jnp.dot(hi, e, preferred_element_type=F32)
                + jnp.dot(lo, e, preferred_element_type=F32))

    dt_x = expand(dt)
    wend_x = expand(w_end)
    ea_x = expand(ea)
    xs = xbc_ref[:, 0:D_INNER]
    xg_ref[...] = (xs * dt_x).astype(BF16)
    xd_ref[...] = xs * wend_x

    for g in range(N_GROUPS):
        gcols = slice(g * GROUP_W, (g + 1) * GROUP_W)
        b_g = xbc_ref[:, D_INNER + g * D_STATE:D_INNER + (g + 1) * D_STATE].astype(BF16)
        c_g = xbc_ref[:, D_INNER + N_GROUPS * D_STATE + g * D_STATE:
                      D_INNER + N_GROUPS * D_STATE + (g + 1) * D_STATE].astype(BF16)
        cb = lax.dot_general(c_g, b_g, (((1,), (1,)), ((), ())), preferred_element_type=F32)
        s_g = ssm_ref[0, g * GROUP_W:(g + 1) * GROUP_W, :]
        y_off = lax.dot_general(c_g, s_g.astype(BF16), (((1,), (1,)), ((), ())),
                                preferred_element_type=F32) * ea_x[:, gcols]
        y_parts = []
        for hh in range(GROUP_W // SSM_HEAD_DIM):
            h = g * (GROUP_W // SSM_HEAD_DIM) + hh
            seg = a_cum[:, h:h + 1] - a_cum_t[h:h + 1, :]
            m = (cb * jnp.where(tril, jnp.exp(seg), 0.0)).astype(BF16)
            y_parts.append(jnp.dot(m, xg_ref[:, h * SSM_HEAD_DIM:(h + 1) * SSM_HEAD_DIM],
                                   preferred_element_type=F32))
        y = jnp.concatenate(y_parts, axis=1) + y_off
        y = y + xbc_ref[:, gcols] * dskip_ref[:, gcols]
        xd_t = xd_ref[:, gcols].T.astype(BF16)
        contrib = jnp.dot(xd_t, b_g, preferred_element_type=F32)
        for hh in range(GROUP_W // SSM_HEAD_DIM):
            h = g * (GROUP_W // SSM_HEAD_DIM) + hh
            rows = slice(hh * SSM_HEAD_DIM, (hh + 1) * SSM_HEAD_DIM)
            ssm_ref[0, g * GROUP_W + hh * SSM_HEAD_DIM:g * GROUP_W + (hh + 1) * SSM_HEAD_DIM, :] = (
                s_g[rows, :] * chunk_decay_t[h:h + 1, :] + contrib[rows, :])
        z = zx_ref[:, gcols]
        yz = y * (z * jax.nn.sigmoid(z))
        ms = jnp.mean(yz * yz, axis=-1, keepdims=True)
        y_ref[:, gcols] = (yz * lax.rsqrt(ms + RMS_EPS) * gn_ref[:, gcols]).astype(y_ref.dtype)


def _ssd(zx, dt_raw, conv0, ssm0, cw, cb, dtb, alog, dskip, gn, expand_mat, *, n_seq, valid_len):
    rows = zx.shape[0]
    cps = rows // n_seq // SSD_CHUNK
    width = zx.shape[1]
    const = lambda shape: pl.BlockSpec(shape, lambda b, c: (0,) * len(shape))
    return pl.pallas_call(
        functools.partial(_ssd_body, valid_len=valid_len),
        grid=(n_seq, cps),
        in_specs=[
            pl.BlockSpec((SSD_CHUNK, width), lambda b, c: (b * cps + c, 0)),
            pl.BlockSpec((SSD_CHUNK, LANES), lambda b, c: (b * cps + c, 0)),
            pl.BlockSpec((1, SUBLANES, CONV_DIM), lambda b, c: (b, 0, 0)),
            pl.BlockSpec((1, D_INNER, D_STATE), lambda b, c: (b, 0, 0)),
            const((SUBLANES, CONV_DIM)), const((1, CONV_DIM)), const((1, LANES)), const((1, LANES)),
            const((1, D_INNER)), const((1, D_INNER)), const((LANES, D_INNER)),
        ],
        out_specs=[
            pl.BlockSpec((SSD_CHUNK, D_INNER), lambda b, c: (b * cps + c, 0)),
            pl.BlockSpec((1, SUBLANES, CONV_DIM), lambda b, c: (b, 0, 0)),
            pl.BlockSpec((1, D_INNER, D_STATE), lambda b, c: (b, 0, 0)),
        ],
        out_shape=[
            jax.ShapeDtypeStruct((rows, D_INNER), BF16),
            jax.ShapeDtypeStruct((n_seq, SUBLANES, CONV_DIM), F32),
            jax.ShapeDtypeStruct((n_seq, D_INNER, D_STATE), F32),
        ],
        scratch_shapes=[
            pltpu.VMEM((SSD_CHUNK + SUBLANES, CONV_DIM), F32),
            pltpu.VMEM((SSD_CHUNK, CONV_DIM), F32),
            pltpu.VMEM((SSD_CHUNK, D_INNER), BF16),
            pltpu.VMEM((SSD_CHUNK, D_INNER), F32),
        ],
        compiler_params=_cparams("parallel", "arbitrary"),
        name="ssd_mixer",
    )(zx, dt_raw, conv0, ssm0, cw, cb, dtb, alog, dskip, gn, expand_mat)


def _router_body(x_ref, g_ref, wr_ref, br_ref, cnt0_ref,
                 xn_ref, idx_ref, gate_ref, pos_ref, cnt_ref, cnt_scr):
    tm = x_ref.shape[0]

    @pl.when(pl.program_id(0) == 0)
    def _():
        cnt_scr[...] = cnt0_ref[...]

    x = x_ref[...]
    ms = jnp.mean(x * x, axis=-1, keepdims=True)
    xh = x * lax.rsqrt(ms + RMS_EPS) * g_ref[...]
    xn_ref[...] = xh.astype(xn_ref.dtype)
    logits = jnp.dot(xh, wr_ref[...], preferred_element_type=F32, precision=HIGHEST) + br_ref[...]
    lane = lax.broadcasted_iota(jnp.int32, (tm, LANES), 1)
    lane_f = lane.astype(F32)
    work = logits
    vals, idxs, hots = [], [], []
    for _ in range(TOP_K):
        m = jnp.max(work, axis=-1, keepdims=True)
        am = jnp.min(jnp.where(work == m, lane_f, float(LANES)), axis=-1, keepdims=True)
        hot = lane_f == am
        vals.append(m)
        idxs.append(am)
        hots.append(hot)
        work = jnp.where(hot, -jnp.inf, work)
    es = [jnp.exp(v - vals[0]) for v in vals]
    den = es[0] + es[1] + es[2] + es[3]
    sel = (hots[0] | hots[1] | hots[2] | hots[3])
    sel_f = sel.astype(F32)
    before = (lax.broadcasted_iota(jnp.int32, (tm, tm), 0)
              > lax.broadcasted_iota(jnp.int32, (tm, tm), 1)).astype(BF16)
    rank = jnp.dot(before, sel_f.astype(BF16), preferred_element_type=F32) + cnt_scr[0:1, :]
    cnt_new = cnt_scr[0:1, :] + jnp.sum(sel_f, axis=0, keepdims=True)
    cnt_scr[...] = jnp.broadcast_to(cnt_new, cnt_scr.shape)
    cnt_ref[...] = jnp.broadcast_to(cnt_new, cnt_ref.shape)
    idx_o = jnp.zeros((tm, LANES), jnp.int32)
    gate_o = jnp.zeros((tm, LANES), F32)
    pos_o = jnp.zeros((tm, LANES), jnp.int32)
    for k in range(TOP_K):
        p_k = jnp.sum(jnp.where(hots[k], rank, 0.0), axis=-1, keepdims=True).astype(jnp.int32)
        idx_o = jnp.where(lane == k, idxs[k].astype(jnp.int32), idx_o)
        gate_o = jnp.where(lane == k, es[k] / den, gate_o)
        pos_o = jnp.where(lane == k, p_k, pos_o)
    idx_ref[...] = idx_o
    gate_ref[...] = gate_o
    pos_ref[...] = pos_o


def _router(x, g, wr, br, cnt0, *, tm):
    t = x.shape[0]
    row = lambda w, d: (pl.BlockSpec((tm, w), lambda i: (i, 0)), jax.ShapeDtypeStruct((t, w), d))
    outs = [row(D_MODEL, BF16), row(LANES, jnp.int32), row(LANES, F32), row(LANES, jnp.int32)]
    const = lambda shape: pl.BlockSpec(shape, lambda i: (0,) * len(shape))
    return pl.pallas_call(
        _router_body,
        grid=(t // tm,),
        in_specs=[pl.BlockSpec((tm, D_MODEL), lambda i: (i, 0)),
                  const((1, D_MODEL)), const((D_MODEL, LANES)), const((1, LANES)),
                  const((SUBLANES, LANES))],
        out_specs=[o[0] for o in outs] + [const((SUBLANES, LANES))],
        out_shape=[o[1] for o in outs] + [jax.ShapeDtypeStruct((SUBLANES, LANES), F32)],
        scratch_shapes=[pltpu.VMEM((SUBLANES, LANES), F32)],
        compiler_params=_cparams("arbitrary"),
        name="moe_router",
    )(x, g, wr, br, cnt0)


def _experts_body(be_ref, nused_ref, x_ref, wgu_ref, bgu_ref, wd_ref, bd_ref, o_ref):
    i = pl.program_id(0)

    @pl.when(i < nused_ref[0])
    def _():
        gu = jnp.dot(x_ref[...], wgu_ref[0], preferred_element_type=F32) + bgu_ref[0]
        x_glu = jnp.minimum(gu[:, :D_EXPERT], SWIGLU_LIMIT)
        x_lin = jnp.clip(gu[:, D_EXPERT:], -SWIGLU_LIMIT, SWIGLU_LIMIT)
        act = (x_lin + 1.0) * (x_glu * jax.nn.sigmoid(SWIGLU_ALPHA * x_glu))
        o_ref[...] = (jnp.dot(act.astype(BF16), wd_ref[0], preferred_element_type=F32)
                      + bd_ref[0]).astype(o_ref.dtype)

    @pl.when(i >= nused_ref[0])
    def _():
        o_ref[...] = jnp.zeros(o_ref.shape, o_ref.dtype)


def _experts(x_sorted, block_expert, n_used, w_gu, b_gu, w_down, b_down):
    n_slots = x_sorted.shape[0]
    n_blocks = n_slots // MOE_BM
    grid_spec = pltpu.PrefetchScalarGridSpec(
        num_scalar_prefetch=2,
        grid=(n_blocks,),
        in_specs=[
            pl.BlockSpec((MOE_BM, D_MODEL), lambda i, be, nu: (i, 0)),
            pl.BlockSpec((1, D_MODEL, 2 * D_EXPERT), lambda i, be, nu: (be[i], 0, 0)),
            pl.BlockSpec((1, 1, 2 * D_EXPERT), lambda i, be, nu: (be[i], 0, 0)),
            pl.BlockSpec((1, D_EXPERT, D_MODEL), lambda i, be, nu: (be[i], 0, 0)),
            pl.BlockSpec((1, 1, D_MODEL), lambda i, be, nu: (be[i], 0, 0)),
        ],
        out_specs=pl.BlockSpec((MOE_BM, D_MODEL), lambda i, be, nu: (i, 0)),
    )
    return pl.pallas_call(
        _experts_body,
        grid_spec=grid_spec,
        out_shape=jax.ShapeDtypeStruct((n_slots, D_MODEL), F32),
        compiler_params=_cparams("arbitrary"),
        name="moe_experts",
    )(block_expert, n_used, x_sorted, w_gu, b_gu, w_down, b_down)


def _moe(h_streams, g, wr, br, w_gu, b_gu, w_down, b_down):
    cnt = jnp.zeros((SUBLANES, LANES), F32)
    routed = []
    for h in h_streams:
        xn, idx, gate, pos, cnt = _router(h, g, wr, br, cnt, tm=256)
        routed.append((xn, idx[:, :TOP_K], gate[:, :TOP_K], pos[:, :TOP_K]))
    counts = cnt[0, :N_EXPERTS].astype(jnp.int32)
    padded = (counts + MOE_BM - 1) // MOE_BM * MOE_BM
    pad_end = jnp.cumsum(padded)
    pad_start = pad_end - padded
    t_all = sum(h.shape[0] for h in h_streams)
    n_blocks = -(-(t_all * TOP_K + N_EXPERTS * (MOE_BM - 1)) // MOE_BM)
    n_slots = n_blocks * MOE_BM
    n_used = (pad_end[-1] // MOE_BM).astype(jnp.int32)
    blk = jnp.minimum(jnp.arange(n_blocks, dtype=jnp.int32), n_used - 1) * MOE_BM
    block_expert = jnp.minimum(jnp.searchsorted(pad_end, blk, side='right'),
                               N_EXPERTS - 1).astype(jnp.int32)
    slots = [pad_start[idx] + pos for (_, idx, _, pos) in routed]
    slots_flat = jnp.concatenate([s.reshape(-1) for s in slots])
    tok = jnp.arange(t_all * TOP_K, dtype=jnp.int32) // TOP_K
    slot_token = jnp.zeros((n_slots,), jnp.int32).at[slots_flat].set(tok, unique_indices=True)
    xn_all = jnp.concatenate([r[0] for r in routed], axis=0)
    x_sorted = jnp.take(xn_all, slot_token, axis=0)
    y_slots = _experts(x_sorted, block_expert, n_used.reshape(1), w_gu, b_gu, w_down, b_down)
    outs = []
    for h, s, (_, _, gate, _) in zip(h_streams, slots, routed):
        y = jnp.einsum('tk,tkd->td', gate, jnp.take(y_slots, s, axis=0))
        outs.append(h + y)
    return outs


def _sb_block(z, v, r_in, suffix_mat, mask):
    sp = _softplus(z)
    if mask is not None:
        sp = jnp.where(mask, sp, 0.0)
    hi, lo = _split_bf16(sp)
    s_incl = (jnp.dot(hi, suffix_mat, preferred_element_type=F32)
              + jnp.dot(lo, suffix_mat, preferred_element_type=F32)) + r_in
    w = jnp.exp(z - s_incl)
    if mask is not None:
        w = jnp.where(mask, w, 0.0)
    return jnp.dot(w.astype(BF16), v, preferred_element_type=F32), s_incl[:, 0:1]


def _suffix_matrix(n):
    return (lax.broadcasted_iota(jnp.int32, (n, n), 0)
            >= lax.broadcasted_iota(jnp.int32, (n, n), 1)).astype(BF16)


def _attn_prompt_body(bias_ref, q_ref, k_ref, v_ref, o_ref):
    hp = pl.program_id(1)
    qi = pl.program_id(2)
    tq, tk = ATTN_TQ, ATTN_TK
    q = q_ref[...]
    lane = lax.broadcasted_iota(jnp.int32, (tq, LANES), 1)
    first = lane < HEAD_DIM
    suffix = _suffix_matrix(tk)
    diag_mask = (lax.broadcasted_iota(jnp.int32, (tq, tk), 1)
                 < lax.broadcasted_iota(jnp.int32, (tq, tk), 0))
    accs = []
    for hh in range(2):
        qh = jnp.where(first if hh == 0 else ~first, q, jnp.zeros_like(q))
        bias = bias_ref[2 * hp + hh]

        def block(j, carry, mask):
            r, acc = carry
            start = pl.multiple_of(j * tk, tk)
            kb = k_ref[pl.ds(start, tk), :]
            vb = v_ref[pl.ds(start, tk), :]
            z = lax.dot_general(qh, kb, (((1,), (1,)), ((), ())),
                                preferred_element_type=F32) + bias
            contrib, r = _sb_block(z, vb, r, suffix, mask)
            return r, acc + contrib

        carry = (jnp.zeros((tq, 1), F32), jnp.zeros((tq, LANES), F32))
        carry = block(qi, carry, diag_mask)
        carry = lax.fori_loop(0, qi, lambda t, c: block(qi - 1 - t, c, None), carry)
        accs.append(carry[1])
    o_ref[...] = jnp.where(first, accs[0], accs[1]).astype(o_ref.dtype)


def _attn_prompt(q, k, v, bias, *, n_seq):
    t = q.shape[0]
    l = t // n_seq
    assert ATTN_TQ == ATTN_TK and l % ATTN_TQ == 0
    nq = l // ATTN_TQ
    grid_spec = pltpu.PrefetchScalarGridSpec(
        num_scalar_prefetch=0,
        grid=(n_seq, N_HEADS // 2, nq),
        in_specs=[
            pl.BlockSpec(memory_space=pltpu.SMEM),
            pl.BlockSpec((ATTN_TQ, LANES), lambda b, hp, qi: (b * nq + qi, hp)),
            pl.BlockSpec((l, LANES), lambda b, hp, qi: (b, hp)),
            pl.BlockSpec((l, LANES), lambda b, hp, qi: (b, hp)),
        ],
        out_specs=pl.BlockSpec((ATTN_TQ, LANES), lambda b, hp, qi: (b * nq + qi, hp)),
    )
    return pl.pallas_call(
        _attn_prompt_body,
        grid_spec=grid_spec,
        out_shape=jax.ShapeDtypeStruct((t, N_HEADS * HEAD_DIM), BF16),
        compiler_params=_cparams("parallel", "parallel", "arbitrary"),
        name="sb_attn_prompt",
    )(bias, q, k, v)


def _attn_sample_body(pt_ref, qbd_ref, kn_ref, vn_ref, bias_ref, *refs):
    pages = PAGES_PER_STEP
    k_refs = refs[:pages]
    v_refs = refs[pages:2 * pages]
    o_ref = refs[2 * pages]
    acc_ref, r_ref = refs[2 * pages + 1:]
    s = pl.program_id(1)
    n_rows = qbd_ref.shape[1]
    n_new = n_rows // N_HEADS
    qbd = qbd_ref[0]
    bias = bias_ref[...]
    suffix = _suffix_matrix(PAGE_SIZE)

    def logits(kb):
        return lax.dot_general(qbd, kb, (((1,), (1,)), ((), ())), preferred_element_type=F32) + bias

    @pl.when(s == 0)
    def _():
        col = lax.broadcasted_iota(jnp.int32, (n_rows, PAGE_SIZE), 1)
        t_of_row = lax.broadcasted_iota(jnp.int32, (n_rows, PAGE_SIZE), 0) % n_new
        contrib, r = _sb_block(logits(kn_ref[0]), vn_ref[0], jnp.zeros((n_rows, 1), F32),
                               suffix, col < t_of_row)
        acc_ref[...] = contrib
        r_ref[...] = jnp.broadcast_to(r, r_ref.shape)

    for i in range(pages):
        contrib, r = _sb_block(logits(k_refs[i][0].astype(BF16)), v_refs[i][0].astype(BF16),
                               r_ref[:, 0:1], suffix, None)
        acc_ref[...] += contrib
        r_ref[...] = jnp.broadcast_to(r, r_ref.shape)

    @pl.when(s == pl.num_programs(1) - 1)
    def _():
        lane_head = lax.broadcasted_iota(jnp.int32, (n_new, N_HEADS * HEAD_DIM), 1) // HEAD_DIM
        out = jnp.zeros((n_new, N_HEADS * HEAD_DIM), F32)
        for h in range(N_HEADS):
            out = jnp.where(lane_head == h, acc_ref[h * n_new:(h + 1) * n_new, :], out)
        o_ref[0] = out.astype(o_ref.dtype)


def _attn_sample(q, k_new, v_new, bias, cache_k, cache_v, page_table):
    b, n_new, d = q.shape
    n_pages = page_table.shape[1]
    pages = PAGES_PER_STEP
    assert n_pages % pages == 0 and n_new <= PAGE_SIZE
    n_rows = N_HEADS * n_new
    q4 = q.reshape(b, n_new, N_HEADS, HEAD_DIM).transpose(0, 2, 1, 3)
    eye = jnp.eye(N_HEADS, dtype=q.dtype)
    qbd = (q4[:, :, :, None, :] * eye[None, :, None, :, None]).reshape(b, n_rows, d)
    pad = lambda x: jnp.pad(x, ((0, 0), (0, PAGE_SIZE - n_new), (0, 0)))
    bias_rows = jnp.broadcast_to(jnp.repeat(bias, n_new)[:, None], (n_rows, PAGE_SIZE)).astype(F32)

    def page_spec(i):
        return pl.BlockSpec((1, PAGE_SIZE, d),
                            lambda bb, s, pt: (pt[bb, n_pages - 1 - (s * pages + i)], 0, 0))

    grid_spec = pltpu.PrefetchScalarGridSpec(
        num_scalar_prefetch=1,
        grid=(b, n_pages // pages),
        in_specs=[
            pl.BlockSpec((1, n_rows, d), lambda bb, s, pt: (bb, 0, 0)),
            pl.BlockSpec((1, PAGE_SIZE, d), lambda bb, s, pt: (bb, 0, 0)),
            pl.BlockSpec((1, PAGE_SIZE, d), lambda bb, s, pt: (bb, 0, 0)),
            pl.BlockSpec((n_rows, PAGE_SIZE), lambda bb, s, pt: (0, 0)),
        ] + [page_spec(i) for i in range(pages)] + [page_spec(i) for i in range(pages)],
        out_specs=pl.BlockSpec((1, n_new, d), lambda bb, s, pt: (bb, 0, 0)),
        scratch_shapes=[pltpu.VMEM((n_rows, d), F32), pltpu.VMEM((n_rows, LANES), F32)],
    )
    return pl.pallas_call(
        _attn_sample_body,
        grid_spec=grid_spec,
        out_shape=jax.ShapeDtypeStruct((b, n_new, d), BF16),
        compiler_params=_cparams("parallel", "arbitrary"),
        name="sb_attn_sample",
    )(page_table, qbd, pad(k_new), pad(v_new), bias_rows,
      *([cache_k] * pages), *([cache_v] * pages))


def _rmsnorm_body(x_ref, g_ref, o_ref):
    x = x_ref[...]
    ms = jnp.mean(x * x, axis=-1, keepdims=True)
    o_ref[...] = x * lax.rsqrt(ms + RMS_EPS) * g_ref[...]


def _rmsnorm(x, g, *, tm):
    t, d = x.shape
    return pl.pallas_call(
        _rmsnorm_body,
        grid=(t // tm,),
        in_specs=[pl.BlockSpec((tm, d), lambda i: (i, 0)), pl.BlockSpec((1, d), lambda i: (0, 0))],
        out_specs=pl.BlockSpec((tm, d), lambda i: (i, 0)),
        out_shape=jax.ShapeDtypeStruct((t, d), F32),
        compiler_params=_cparams("parallel"),
        name="final_rmsnorm",
    )(x, g)


def _pad_lanes(x, n=LANES, value=0.0):
    return jnp.pad(x, [(0, 0)] * (x.ndim - 1) + [(0, n - x.shape[-1])], constant_values=value)


def kernel(x_prompt, x_sample, state_conv, state_ssm, cache_k, cache_v, page_table, g_mix, w_in, conv_w, conv_b, dt_bias, a_log, d_skip, g_ssm_norm, w_ssm_out, g_kv, w_k, w_v, g_attn, w_q, w_o, sb_bias, g_ffn, w_router, b_router, w_gu, b_gu, w_down, b_down, g_final):
    bp, lp, d = x_prompt.shape
    bs, ls, _ = x_sample.shape
    tp, ts = bp * lp, bs * ls
    streams = [x_prompt.reshape(tp, d), x_sample.reshape(ts, d)]
    tms = [1024, ts]

    w_in0 = w_in[0] * g_mix[0][:, None]
    w_zx = w_in0[:, :D_INNER + CONV_DIM].astype(BF16)
    w_dt = _pad_lanes(w_in0[:, D_INNER + CONV_DIM:])
    w_out = w_ssm_out[0].astype(BF16)
    w_q1 = (w_q[0] * g_attn[0][:, None] * (HEAD_DIM ** -0.5)).astype(BF16)
    w_k1 = (w_k * g_kv[:, None]).astype(BF16)
    w_v1 = (w_v * g_kv[:, None]).astype(BF16)
    w_o1 = w_o[0].astype(BF16)
    w_gu_b = w_gu.astype(BF16)
    w_down_b = w_down.astype(BF16)
    wr = _pad_lanes(w_router)
    br = _pad_lanes(b_router, value=NEG_BIG)[:, None, :]
    cw = jnp.pad(conv_w[0], ((0, SUBLANES - CONV_W), (0, 0)))
    expand_mat = (jnp.arange(D_INNER)[None, :] // SSM_HEAD_DIM
                  == jnp.arange(LANES)[:, None]).astype(BF16)
    dskip_x = jnp.repeat(d_skip[0], SSM_HEAD_DIM)[None, :]
    conv_pad = lambda cs: jnp.pad(cs, ((0, 0), (SUBLANES - (CONV_W - 1), 0), (0, 0)))
    conv0 = [jnp.zeros((bp, SUBLANES, CONV_DIM), F32), conv_pad(state_conv[0])]
    ssm0 = [jnp.zeros((bp, D_INNER, D_STATE), F32), state_ssm[0].reshape(bs, D_INNER, D_STATE)]

    new_conv, new_ssm, h1 = [], [], []
    for i, (h, tm) in enumerate(zip(streams, tms)):
        n_seq, l = (bp, lp) if i == 0 else (bs, ls)
        zx, = _norm_matmul(h, w_zx, norm=True, tm=tm, tn=1024, name="in_proj")
        dt_raw, = _norm_matmul(h, w_dt, norm=True, tm=tm, tn=LANES, precision=HIGHEST,
                               name="in_proj_dt")
        if l % SSD_CHUNK:
            assert l < SSD_CHUNK and l % SUBLANES == 0
            padrows = lambda x: jnp.pad(x.reshape(n_seq, l, -1), ((0, 0), (0, SSD_CHUNK - l), (0, 0))
                                        ).reshape(n_seq * SSD_CHUNK, -1)
            zx, dt_raw = padrows(zx), padrows(dt_raw)
            valid = l
        else:
            valid = SSD_CHUNK
        y, conv_o, ssm_o = _ssd(zx, dt_raw, conv0[i], ssm0[i], cw, conv_b[0][None, :],
                                _pad_lanes(dt_bias[0])[None, :], _pad_lanes(a_log[0])[None, :],
                                dskip_x, g_ssm_norm[0][None, :], expand_mat,
                                n_seq=n_seq, valid_len=valid)
        if l % SSD_CHUNK:
            y = y.reshape(n_seq, SSD_CHUNK, D_INNER)[:, :l].reshape(n_seq * l, D_INNER)
        new_conv.append(conv_o[:, SUBLANES - (CONV_W - 1):][None])
        new_ssm.append(ssm_o.reshape(n_seq, N_SSM_HEADS, SSM_HEAD_DIM, D_STATE)[None])
        hm, = _norm_matmul(y, w_out, norm=False, res=h, tm=tm, tn=1024, name="ssm_out_proj")
        h1.append(hm)
    h1 = _moe(h1, g_ffn[0][None, :], wr[0], br[0], w_gu_b[0], b_gu[0][:, None, :],
              w_down_b[0], b_down[0][:, None, :])

    qs, ks, vs, kbs, vbs = [], [], [], [], []
    for h, tm in zip(h1, tms):
        q, = _norm_matmul(h, w_q1, norm=True, out_dtypes=(BF16,), tm=tm, tn=1024, name="q_proj")
        k, kb = _norm_matmul(h, w_k1, norm=True, out_dtypes=(F32, BF16), tm=tm, tn=1024,
                             name="k_proj")
        v, vb = _norm_matmul(h, w_v1, norm=True, out_dtypes=(F32, BF16), tm=tm, tn=1024,
                             name="v_proj")
        qs.append(q); ks.append(k); vs.append(v); kbs.append(kb); vbs.append(vb)
    o_p = _attn_prompt(qs[0], kbs[0], vbs[0], sb_bias[0], n_seq=bp)
    n_phys = cache_k.shape[0]
    o_s = _attn_sample(qs[1].reshape(bs, ls, d), kbs[1].reshape(bs, ls, d), vbs[1].reshape(bs, ls, d),
                       sb_bias[0], cache_k.reshape(n_phys, PAGE_SIZE, d),
                       cache_v.reshape(n_phys, PAGE_SIZE, d), page_table).reshape(ts, d)
    h2 = []
    for h, o, tm in zip(h1, [o_p, o_s], tms):
        hm, = _norm_matmul(o, w_o1, norm=False, res=h, tm=tm, tn=1024, name="attn_out_proj")
        h2.append(hm)
    h2 = _moe(h2, g_ffn[1][None, :], wr[1], br[1], w_gu_b[1], b_gu[1][:, None, :],
              w_down_b[1], b_down[1][:, None, :])

    y_p = _rmsnorm(h2[0], g_final[None, :], tm=1024).reshape(bp, lp, d)
    y_s = _rmsnorm(h2[1], g_final[None, :], tm=ts).reshape(bs, ls, d)
    kv_shape = lambda x, b, l: x.reshape(b, l, N_HEADS, HEAD_DIM)
    return (y_p, y_s, new_conv[0], new_ssm[0], kv_shape(ks[0], bp, lp), kv_shape(vs[0], bp, lp),
            new_conv[1], new_ssm[1], kv_shape(ks[1], bs, ls), kv_shape(vs[1], bs, ls))
```

```python
import functools
import math

import jax
import jax.numpy as jnp
from jax import lax
from jax.experimental import pallas as pl
from jax.experimental.pallas import tpu as pltpu

F32 = jnp.float32
BF16 = jnp.bfloat16
HIGHEST = lax.Precision.HIGHEST

LANES = 128
SUBLANES = 8
VMEM_LIMIT = 48 * 1024 * 1024

D_MODEL = 1024
D_INNER = 2048
SSM_HEAD_DIM = 64
N_SSM_HEADS = 32
D_STATE = 128
N_GROUPS = 8
GROUP_W = D_INNER // N_GROUPS
CONV_W = 4
CONV_DIM = D_INNER + 2 * N_GROUPS * D_STATE
SSD_CHUNK = 128
HEAD_DIM = 64
N_HEADS = 16
N_EXPERTS = 32
TOP_K = 4
D_EXPERT = 1024
SWIGLU_ALPHA = 1.702
SWIGLU_LIMIT = 7.0
RMS_EPS = 1e-5
PAGE_SIZE = 128

MOE_BM = 256
ATTN_TQ = 256
ATTN_TK = 256
ATTN_UNROLL = 8
PAGES_PER_STEP = 4
NEG_BIG = -1e30


def _cparams(*sem, flags=None):
    return pltpu.CompilerParams(dimension_semantics=sem, vmem_limit_bytes=VMEM_LIMIT, flags=flags)


def _softplus(z):
    return jnp.maximum(z, 0.0) + jnp.log1p(jnp.exp(-jnp.abs(z)))


def _split_bf16(x):
    hi = x.astype(BF16)
    lo = (x - hi.astype(F32)).astype(BF16)
    return hi, lo


def _nm_body(*refs, norm, has_res, n_out, precision):
    x_ref, w_ref = refs[0], refs[1]
    pos = 2
    res_ref = None
    if has_res:
        res_ref = refs[pos]
        pos += 1
    out_refs = refs[pos:pos + n_out]
    pos += n_out
    if norm:
        xn_ref = refs[pos]

        @pl.when(pl.program_id(1) == 0)
        def _():
            x = x_ref[...].astype(F32)
            ms = jnp.mean(x * x, axis=-1, keepdims=True)
            xn_ref[...] = (x * lax.rsqrt(ms + RMS_EPS)).astype(xn_ref.dtype)

        lhs = xn_ref[...]
    else:
        lhs = x_ref[...]
    acc = jnp.dot(lhs, w_ref[...], preferred_element_type=F32, precision=precision)
    if has_res:
        acc = acc + res_ref[...]
    for o in out_refs:
        o[...] = acc.astype(o.dtype)


def _norm_matmul(x, w, *, norm, res=None, out_dtypes=(F32,), tm, tn, precision=None, name="matmul"):
    t, k = x.shape
    n = w.shape[1]
    assert t % tm == 0 and n % tn == 0
    in_specs = [pl.BlockSpec((tm, k), lambda i, j: (i, 0)),
                pl.BlockSpec((k, tn), lambda i, j: (0, j))]
    args = [x, w]
    if res is not None:
        in_specs.append(pl.BlockSpec((tm, tn), lambda i, j: (i, j)))
        args.append(res)
    outs = pl.pallas_call(
        functools.partial(_nm_body, norm=norm, has_res=res is not None,
                          n_out=len(out_dtypes), precision=precision),
        grid=(t // tm, n // tn),
        in_specs=in_specs,
        out_specs=[pl.BlockSpec((tm, tn), lambda i, j: (i, j)) for _ in out_dtypes],
        out_shape=[jax.ShapeDtypeStruct((t, n), d) for d in out_dtypes],
        scratch_shapes=[pltpu.VMEM((tm, k), w.dtype)] if norm else [],
        compiler_params=_cparams("parallel", "arbitrary"),
        name=name,
    )(*args)
    return outs


def _ssd_body(zx_ref, dt_ref, conv0_ref, ssm0_ref, cw_ref, cb_ref, dtb_ref, alog_ref,
              dskip_ref, gn_ref, e_ref,
              y_ref, conv_out_ref, ssm_ref,
              ext_ref, xbc_ref, xg_ref, xd_ref, *, valid_len):
    L = SSD_CHUNK
    c = pl.program_id(1)

    @pl.when(c == 0)
    def _():
        ext_ref[0:SUBLANES, :] = conv0_ref[0]
        ssm_ref[0] = ssm0_ref[0]

    ext_ref[SUBLANES:SUBLANES + L, :] = zx_ref[:, D_INNER:D_INNER + CONV_DIM]
    strip = 512
    for s in range(CONV_DIM // strip):
        cols = slice(s * strip, (s + 1) * strip)
        acc = jnp.broadcast_to(cb_ref[:, cols], (L, strip))
        for w in range(CONV_W):
            r0 = SUBLANES - (CONV_W - 1) + w
            acc = acc + ext_ref[r0:r0 + L, cols] * cw_ref[w:w + 1, cols]
        xbc_ref[:, cols] = acc * jax.nn.sigmoid(acc)
    conv_out_ref[0] = ext_ref[valid_len:valid_len + SUBLANES, :]
    ext_ref[0:SUBLANES, :] = ext_ref[L:L + SUBLANES, :]

    lane = lax.broadcasted_iota(jnp.int32, (L, LANES), 1)
    row = lax.broadcasted_iota(jnp.int32, (L, LANES), 0)
    live = (lane < N_SSM_HEADS) & (row < valid_len)
    dt = jnp.where(live, _softplus(dt_ref[...] + dtb_ref[...]), 0.0)
    a = dt * (-jnp.exp(alog_ref[...]))
    tril = (lax.broadcasted_iota(jnp.int32, (L, L), 0)
            >= lax.broadcasted_iota(jnp.int32, (L, L), 1))
    a_cum = jnp.dot(tril.astype(F32), a, preferred_element_type=F32, precision=HIGHEST)
    a_cum_t = a_cum.T
    a_last = a_cum[L - 1:L, :]
    ea = jnp.exp(a_cum)
    w_end = dt * jnp.exp(a_last - a_cum)
    chunk_decay_t = jnp.exp(a_cum_t[:, L - 1:L])

    def expand(p):
        hi, lo = _split_bf16(p)
        e = e_ref[...]
        return (jnp.dot(hi, e, preferred_element_type=F32)
                + jnp.dot(lo, e, preferred_element_type=F32))

    dt_x = expand(dt)
    wend_x = expand(w_end)
    ea_x = expand(ea)
    xs = xbc_ref[:, 0:D_INNER]
    xg_ref[...] = (xs * dt_x).astype(BF16)
    xd_ref[...] = xs * wend_x

    for g in range(N_GROUPS):
        gcols = slice(g * GROUP_W, (g + 1) * GROUP_W)
        b_g = xbc_ref[:, D_INNER + g * D_STATE:D_INNER + (g + 1) * D_STATE].astype(BF16)
        c_g = xbc_ref[:, D_INNER + N_GROUPS * D_STATE + g * D_STATE:
                      D_INNER + N_GROUPS * D_STATE + (g + 1) * D_STATE].astype(BF16)
        cb = lax.dot_general(c_g, b_g, (((1,), (1,)), ((), ())), preferred_element_type=F32)
        s_g = ssm_ref[0, g * GROUP_W:(g + 1) * GROUP_W, :]
        y_off = lax.dot_general(c_g, s_g.astype(BF16), (((1,), (1,)), ((), ())),
                                preferred_element_type=F32) * ea_x[:, gcols]
        y_parts = []
        for hh in range(GROUP_W // SSM_HEAD_DIM):
            h = g * (GROUP_W // SSM_HEAD_DIM) + hh
            seg = a_cum[:, h:h + 1] - a_cum_t[h:h + 1, :]
            m = (cb * jnp.where(tril, jnp.exp(seg), 0.0)).astype(BF16)
            y_parts.append(jnp.dot(m, xg_ref[:, h * SSM_HEAD_DIM:(h + 1) * SSM_HEAD_DIM],
                                   preferred_element_type=F32))
        y = jnp.concatenate(y_parts, axis=1) + y_off
        y = y + xbc_ref[:, gcols] * dskip_ref[:, gcols]
        xd_t = xd_ref[:, gcols].T.astype(BF16)
        contrib = jnp.dot(xd_t, b_g, preferred_element_type=F32)
        for hh in range(GROUP_W // SSM_HEAD_DIM):
            h = g * (GROUP_W // SSM_HEAD_DIM) + hh
            rows = slice(hh * SSM_HEAD_DIM, (hh + 1) * SSM_HEAD_DIM)
            ssm_ref[0, g * GROUP_W + hh * SSM_HEAD_DIM:g * GROUP_W + (hh + 1) * SSM_HEAD_DIM, :] = (
                s_g[rows, :] * chunk_decay_t[h:h + 1, :] + contrib[rows, :])
        z = zx_ref[:, gcols]
        yz = y * (z * jax.nn.sigmoid(z))
        ms = jnp.mean(yz * yz, axis=-1, keepdims=True)
        y_ref[:, gcols] = (yz * lax.rsqrt(ms + RMS_EPS) * gn_ref[:, gcols]).astype(y_ref.dtype)


def _ssd(zx, dt_raw, conv0, ssm0, cw, cb, dtb, alog, dskip, gn, expand_mat, *, n_seq, valid_len):
    rows = zx.shape[0]
    cps = rows // n_seq // SSD_CHUNK
    width = zx.shape[1]
    const = lambda shape: pl.BlockSpec(shape, lambda b, c: (0,) * len(shape))
    return pl.pallas_call(
        functools.partial(_ssd_body, valid_len=valid_len),
        grid=(n_seq, cps),
        in_specs=[
            pl.BlockSpec((SSD_CHUNK, width), lambda b, c: (b * cps + c, 0)),
            pl.BlockSpec((SSD_CHUNK, LANES), lambda b, c: (b * cps + c, 0)),
            pl.BlockSpec((1, SUBLANES, CONV_DIM), lambda b, c: (b, 0, 0)),
            pl.BlockSpec((1, D_INNER, D_STATE), lambda b, c: (b, 0, 0)),
            const((SUBLANES, CONV_DIM)), const((1, CONV_DIM)), const((1, LANES)), const((1, LANES)),
            const((1, D_INNER)), const((1, D_INNER)), const((LANES, D_INNER)),
        ],
        out_specs=[
            pl.BlockSpec((SSD_CHUNK, D_INNER), lambda b, c: (b * cps + c, 0)),
            pl.BlockSpec((1, SUBLANES, CONV_DIM), lambda b, c: (b, 0, 0)),
            pl.BlockSpec((1, D_INNER, D_STATE), lambda b, c: (b, 0, 0)),
        ],
        out_shape=[
            jax.ShapeDtypeStruct((rows, D_INNER), BF16),
            jax.ShapeDtypeStruct((n_seq, SUBLANES, CONV_DIM), F32),
            jax.ShapeDtypeStruct((n_seq, D_INNER, D_STATE), F32),
        ],
        scratch_shapes=[
            pltpu.VMEM((SSD_CHUNK + SUBLANES, CONV_DIM), F32),
            pltpu.VMEM((SSD_CHUNK, CONV_DIM), F32),
            pltpu.VMEM((SSD_CHUNK, D_INNER), BF16),
            pltpu.VMEM((SSD_CHUNK, D_INNER), F32),
        ],
        compiler_params=_cparams("parallel", "arbitrary"),
        name="ssd_mixer",
    )(zx, dt_raw, conv0, ssm0, cw, cb, dtb, alog, dskip, gn, expand_mat)


def _router_body(x_ref, g_ref, wr_ref, br_ref, cnt0_ref,
                 xn_ref, idx_ref, gate_ref, pos_ref, cnt_ref, cnt_scr):
    tm = x_ref.shape[0]

    @pl.when(pl.program_id(0) == 0)
    def _():
        cnt_scr[...] = cnt0_ref[...]

    x = x_ref[...]
    ms = jnp.mean(x * x, axis=-1, keepdims=True)
    xh = x * lax.rsqrt(ms + RMS_EPS) * g_ref[...]
    xn_ref[...] = xh.astype(xn_ref.dtype)
    logits = jnp.dot(xh, wr_ref[...], preferred_element_type=F32, precision=HIGHEST) + br_ref[...]
    lane = lax.broadcasted_iota(jnp.int32, (tm, LANES), 1)
    lane_f = lane.astype(F32)
    work = logits
    vals, idxs, hots = [], [], []
    for _ in range(TOP_K):
        m = jnp.max(work, axis=-1, keepdims=True)
        am = jnp.min(jnp.where(work == m, lane_f, float(LANES)), axis=-1, keepdims=True)
        hot = lane_f == am
        vals.append(m)
        idxs.append(am)
        hots.append(hot)
        work = jnp.where(hot, -jnp.inf, work)
    es = [jnp.exp(v - vals[0]) for v in vals]
    den = es[0] + es[1] + es[2] + es[3]
    sel = (hots[0] | hots[1] | hots[2] | hots[3])
    sel_f = sel.astype(F32)
    before = (lax.broadcasted_iota(jnp.int32, (tm, tm), 0)
              > lax.broadcasted_iota(jnp.int32, (tm, tm), 1)).astype(BF16)
    rank = jnp.dot(before, sel_f.astype(BF16), preferred_element_type=F32) + cnt_scr[0:1, :]
    cnt_new = cnt_scr[0:1, :] + jnp.sum(sel_f, axis=0, keepdims=True)
    cnt_scr[...] = jnp.broadcast_to(cnt_new, cnt_scr.shape)
    cnt_ref[...] = jnp.broadcast_to(cnt_new, cnt_ref.shape)
    idx_o = jnp.zeros((tm, LANES), jnp.int32)
    gate_o = jnp.zeros((tm, LANES), F32)
    pos_o = jnp.zeros((tm, LANES), jnp.int32)
    for k in range(TOP_K):
        p_k = jnp.sum(jnp.where(hots[k], rank, 0.0), axis=-1, keepdims=True).astype(jnp.int32)
        idx_o = jnp.where(lane == k, idxs[k].astype(jnp.int32), idx_o)
        gate_o = jnp.where(lane == k, es[k] / den, gate_o)
        pos_o = jnp.where(lane == k, p_k, pos_o)
    idx_ref[...] = idx_o
    gate_ref[...] = gate_o
    pos_ref[...] = pos_o


def _router(x, g, wr, br, cnt0, *, tm):
    t = x.shape[0]
    row = lambda w, d: (pl.BlockSpec((tm, w), lambda i: (i, 0)), jax.ShapeDtypeStruct((t, w), d))
    outs = [row(D_MODEL, BF16), row(LANES, jnp.int32), row(LANES, F32), row(LANES, jnp.int32)]
    const = lambda shape: pl.BlockSpec(shape, lambda i: (0,) * len(shape))
    return pl.pallas_call(
        _router_body,
        grid=(t // tm,),
        in_specs=[pl.BlockSpec((tm, D_MODEL), lambda i: (i, 0)),
                  const((1, D_MODEL)), const((D_MODEL, LANES)), const((1, LANES)),
                  const((SUBLANES, LANES))],
        out_specs=[o[0] for o in outs] + [const((SUBLANES, LANES))],
        out_shape=[o[1] for o in outs] + [jax.ShapeDtypeStruct((SUBLANES, LANES), F32)],
        scratch_shapes=[pltpu.VMEM((SUBLANES, LANES), F32)],
        compiler_params=_cparams("arbitrary"),
        name="moe_router",
    )(x, g, wr, br, cnt0)


def _experts_body(be_ref, nused_ref, x_ref, wgu_ref, bgu_ref, wd_ref, bd_ref, o_ref):
    i = pl.program_id(0)

    @pl.when(i < nused_ref[0])
    def _():
        gu = jnp.dot(x_ref[...], wgu_ref[0], preferred_element_type=F32) + bgu_ref[0]
        x_glu = jnp.minimum(gu[:, :D_EXPERT], SWIGLU_LIMIT)
        x_lin = jnp.clip(gu[:, D_EXPERT:], -SWIGLU_LIMIT, SWIGLU_LIMIT)
        act = (x_lin + 1.0) * (x_glu * jax.nn.sigmoid(SWIGLU_ALPHA * x_glu))
        o_ref[...] = (jnp.dot(act.astype(BF16), wd_ref[0], preferred_element_type=F32)
                      + bd_ref[0]).astype(o_ref.dtype)

    @pl.when(i >= nused_ref[0])
    def _():
        o_ref[...] = jnp.zeros(o_ref.shape, o_ref.dtype)


def _experts(x_sorted, block_expert, n_used, w_gu, b_gu, w_down, b_down):
    n_slots = x_sorted.shape[0]
    n_blocks = n_slots // MOE_BM
    grid_spec = pltpu.PrefetchScalarGridSpec(
        num_scalar_prefetch=2,
        grid=(n_blocks,),
        in_specs=[
            pl.BlockSpec((MOE_BM, D_MODEL), lambda i, be, nu: (i, 0)),
            pl.BlockSpec((1, D_MODEL, 2 * D_EXPERT), lambda i, be, nu: (be[i], 0, 0)),
            pl.BlockSpec((1, 1, 2 * D_EXPERT), lambda i, be, nu: (be[i], 0, 0)),
            pl.BlockSpec((1, D_EXPERT, D_MODEL), lambda i, be, nu: (be[i], 0, 0)),
            pl.BlockSpec((1, 1, D_MODEL), lambda i, be, nu: (be[i], 0, 0)),
        ],
        out_specs=pl.BlockSpec((MOE_BM, D_MODEL), lambda i, be, nu: (i, 0)),
    )
    return pl.pallas_call(
        _experts_body,
        grid_spec=grid_spec,
        out_shape=jax.ShapeDtypeStruct((n_slots, D_MODEL), F32),
        compiler_params=_cparams("arbitrary"),
        name="moe_experts",
    )(block_expert, n_used, x_sorted, w_gu, b_gu, w_down, b_down)


def _moe(h_streams, g, wr, br, w_gu, b_gu, w_down, b_down):
    cnt = jnp.zeros((SUBLANES, LANES), F32)
    routed = []
    for h in h_streams:
        xn, idx, gate, pos, cnt = _router(h, g, wr, br, cnt, tm=256)
        routed.append((xn, idx[:, :TOP_K], gate[:, :TOP_K], pos[:, :TOP_K]))
    counts = cnt[0, :N_EXPERTS].astype(jnp.int32)
    padded = (counts + MOE_BM - 1) // MOE_BM * MOE_BM
    pad_end = jnp.cumsum(padded)
    pad_start = pad_end - padded
    t_all = sum(h.shape[0] for h in h_streams)
    n_blocks = -(-(t_all * TOP_K + N_EXPERTS * (MOE_BM - 1)) // MOE_BM)
    n_slots = n_blocks * MOE_BM
    n_used = (pad_end[-1] // MOE_BM).astype(jnp.int32)
    blk = jnp.minimum(jnp.arange(n_blocks, dtype=jnp.int32), n_used - 1) * MOE_BM
    block_expert = jnp.minimum(jnp.sum(pad_end[None, :] <= blk[:, None], axis=1),
                               N_EXPERTS - 1).astype(jnp.int32)
    slots = [pad_start[idx] + pos for (_, idx, _, pos) in routed]
    slots_flat = jnp.concatenate([s.reshape(-1) for s in slots])
    tok = jnp.arange(t_all * TOP_K, dtype=jnp.int32) // TOP_K
    slot_token = jnp.zeros((n_slots,), jnp.int32).at[slots_flat].set(tok, unique_indices=True)
    xn_all = jnp.concatenate([r[0] for r in routed], axis=0)
    x_sorted = xn_all.at[slot_token].get(mode='promise_in_bounds')
    y_slots = _experts(x_sorted, block_expert, n_used.reshape(1), w_gu, b_gu, w_down, b_down)
    outs = []
    for h, s, (_, _, gate, _) in zip(h_streams, slots, routed):
        y = jnp.einsum('tk,tkd->td', gate, y_slots.at[s].get(mode='promise_in_bounds'))
        outs.append(h + y)
    return outs


LOG2E = 1.0 / math.log(2.0)


def _softplus2(y):
    neg_abs = lax.bitcast_convert_type(
        lax.bitcast_convert_type(y, jnp.uint32) | jnp.uint32(0x80000000), F32)
    return jnp.maximum(y, 0.0) + jnp.log(1.0 + jnp.exp2(neg_abs)) * LOG2E


def _sb_block(y, v, r_in, suffix_mat, mask):
    sp = _softplus2(y)
    if mask is not None:
        sp = jnp.where(mask, sp, 0.0)
    s_incl = jnp.dot(sp.astype(BF16), suffix_mat, preferred_element_type=F32) + r_in
    w = jnp.exp2(y - s_incl)
    if mask is not None:
        w = jnp.where(mask, w, 0.0)
    return jnp.dot(w.astype(BF16), v, preferred_element_type=F32), s_incl[:, 0:1]


def _suffix_matrix(n):
    return (lax.broadcasted_iota(jnp.int32, (n, n), 0)
            >= lax.broadcasted_iota(jnp.int32, (n, n), 1)).astype(BF16)


def _attn_prompt_body(bias_ref, q_ref, k_ref, v_ref, o_ref, y_scr, sp_scr):
    hp = pl.program_id(1)
    qi = pl.program_id(2)
    tq, tk = ATTN_TQ, ATTN_TK
    q = q_ref[...]
    lane = lax.broadcasted_iota(jnp.int32, (tq, LANES), 1)
    first = lane < HEAD_DIM
    suffix = _suffix_matrix(tk)
    diag_mask = (lax.broadcasted_iota(jnp.int32, (tq, tk), 1)
                 < lax.broadcasted_iota(jnp.int32, (tq, tk), 0))
    heads = ((jnp.where(first, q, jnp.zeros_like(q)), bias_ref[2 * hp]),
             (jnp.where(first, jnp.zeros_like(q), q), bias_ref[2 * hp + 1]))

    def logits(j, masked):
        kb = k_ref[pl.ds(pl.multiple_of(j * tk, tk), tk), :]
        ys = []
        for qh, bias in heads:
            y = lax.dot_general(qh, kb, (((1,), (1,)), ((), ())),
                                preferred_element_type=F32) + bias
            if masked:
                y = jnp.where(diag_mask, y, NEG_BIG)
            ys.append(y)
        return ys

    def split_store(ys, slot):
        for h, y in enumerate(ys):
            y_scr[h, slot] = y
            sp_scr[h, slot] = _softplus2(y).astype(BF16)

    def suffix_sums(slot):
        return [jnp.dot(sp_scr[h, slot], suffix, preferred_element_type=F32) for h in range(2)]

    def finish(s2s, j, slot, carry):
        vb = v_ref[pl.ds(pl.multiple_of(j * tk, tk), tk), :]
        outs, rs = [], []
        for h, s2 in enumerate(s2s):
            s_incl = s2 + carry[h]
            w = jnp.exp2(y_scr[h, slot] - s_incl)
            outs.append(jnp.dot(w.astype(BF16), vb, preferred_element_type=F32))
            rs.append(s_incl[:, 0:1])
        return rs[0], rs[1], carry[2] + jnp.where(first, outs[0], outs[1])

    def step(j_next, j_cur, cur_slot, carry):
        ys = logits(j_next, False)
        s2s = suffix_sums(cur_slot)
        split_store(ys, 1 - cur_slot)
        return finish(s2s, j_cur, cur_slot, carry)

    def steps(j, n, carry):
        for i in range(n):
            carry = step(j - i - 1, j - i, i % 2, carry)
        return carry

    unroll = ATTN_UNROLL
    split_store(logits(qi, True), 0)
    carry = (jnp.zeros((tq, 1), F32), jnp.zeros((tq, 1), F32), jnp.zeros((tq, LANES), F32))
    carry = lax.fori_loop(0, qi // unroll, lambda u, c: steps(qi - unroll * u, unroll, c), carry)
    rest = qi % unroll
    carry = lax.fori_loop(0, rest // 2, lambda u, c: steps(rest - 2 * u, 2, c), carry)
    odd = qi % 2

    def tail(c):
        c = step(0, 1, 0, c)
        return finish(suffix_sums(1), 0, 1, c)

    def last(c):
        return finish(suffix_sums(0), 0, 0, c)

    carry = lax.cond(odd == 1, tail, last, carry)
    o_ref[...] = carry[2].astype(o_ref.dtype)


def _attn_prompt(q, k, v, bias, *, n_seq):
    t = q.shape[0]
    l = t // n_seq
    assert ATTN_TQ == ATTN_TK and l % ATTN_TQ == 0
    nq = l // ATTN_TQ
    grid_spec = pltpu.PrefetchScalarGridSpec(
        num_scalar_prefetch=0,
        grid=(n_seq, N_HEADS // 2, nq),
        in_specs=[
            pl.BlockSpec(memory_space=pltpu.SMEM),
            pl.BlockSpec((ATTN_TQ, LANES), lambda b, hp, qi: (b * nq + qi, hp)),
            pl.BlockSpec((l, LANES), lambda b, hp, qi: (b, hp)),
            pl.BlockSpec((l, LANES), lambda b, hp, qi: (b, hp)),
        ],
        out_specs=pl.BlockSpec((ATTN_TQ, LANES), lambda b, hp, qi: (b * nq + qi, hp)),
        scratch_shapes=[pltpu.VMEM((2, 2, ATTN_TQ, ATTN_TK), F32),
                        pltpu.VMEM((2, 2, ATTN_TQ, ATTN_TK), BF16)],
    )
    return pl.pallas_call(
        _attn_prompt_body,
        grid_spec=grid_spec,
        out_shape=jax.ShapeDtypeStruct((t, N_HEADS * HEAD_DIM), BF16),
        compiler_params=_cparams("parallel", "parallel", "arbitrary"),
        name="sb_attn_prompt",
    )(bias, q, k, v)


def _attn_sample_body(pt_ref, qbd_ref, kn_ref, vn_ref, bias_ref, *refs):
    pages = PAGES_PER_STEP
    k_refs = refs[:pages]
    v_refs = refs[pages:2 * pages]
    o_ref = refs[2 * pages]
    acc_ref, r_ref = refs[2 * pages + 1:]
    s = pl.program_id(1)
    n_rows = qbd_ref.shape[1]
    n_new = n_rows // N_HEADS
    qbd = qbd_ref[0]
    bias = bias_ref[...]
    suffix = _suffix_matrix(PAGE_SIZE)

    def logits(kb):
        return lax.dot_general(qbd, kb, (((1,), (1,)), ((), ())), preferred_element_type=F32) + bias

    @pl.when(s == 0)
    def _():
        col = lax.broadcasted_iota(jnp.int32, (n_rows, PAGE_SIZE), 1)
        t_of_row = lax.broadcasted_iota(jnp.int32, (n_rows, PAGE_SIZE), 0) % n_new
        contrib, r = _sb_block(logits(kn_ref[0]), vn_ref[0], jnp.zeros((n_rows, 1), F32),
                               suffix, col < t_of_row)
        acc_ref[...] = contrib
        r_ref[...] = jnp.broadcast_to(r, r_ref.shape)

    for i in range(pages):
        contrib, r = _sb_block(logits(k_refs[i][0]), v_refs[i][0], r_ref[:, 0:1], suffix, None)
        acc_ref[...] += contrib
        r_ref[...] = jnp.broadcast_to(r, r_ref.shape)

    @pl.when(s == pl.num_programs(1) - 1)
    def _():
        lane_head = lax.broadcasted_iota(jnp.int32, (n_new, N_HEADS * HEAD_DIM), 1) // HEAD_DIM
        out = jnp.zeros((n_new, N_HEADS * HEAD_DIM), F32)
        for h in range(N_HEADS):
            out = jnp.where(lane_head == h, acc_ref[h * n_new:(h + 1) * n_new, :], out)
        o_ref[0] = out.astype(o_ref.dtype)


def _attn_sample(q, k_new, v_new, bias, cache_k, cache_v, page_table):
    b, n_new, d = q.shape
    n_pages = page_table.shape[1]
    pages = PAGES_PER_STEP
    assert n_pages % pages == 0 and n_new <= PAGE_SIZE
    n_rows = N_HEADS * n_new
    q4 = q.reshape(b, n_new, N_HEADS, HEAD_DIM).transpose(0, 2, 1, 3)
    eye = jnp.eye(N_HEADS, dtype=q.dtype)
    qbd = (q4[:, :, :, None, :] * eye[None, :, None, :, None]).reshape(b, n_rows, d)
    pad = lambda x: jnp.pad(x, ((0, 0), (0, PAGE_SIZE - n_new), (0, 0)))
    bias_rows = jnp.broadcast_to(jnp.repeat(bias, n_new)[:, None], (n_rows, PAGE_SIZE)).astype(F32)

    def page_spec(i):
        return pl.BlockSpec((1, PAGE_SIZE, d),
                            lambda bb, s, pt: (pt[bb, n_pages - 1 - (s * pages + i)], 0, 0))

    grid_spec = pltpu.PrefetchScalarGridSpec(
        num_scalar_prefetch=1,
        grid=(b, n_pages // pages),
        in_specs=[
            pl.BlockSpec((1, n_rows, d), lambda bb, s, pt: (bb, 0, 0)),
            pl.BlockSpec((1, PAGE_SIZE, d), lambda bb, s, pt: (bb, 0, 0)),
            pl.BlockSpec((1, PAGE_SIZE, d), lambda bb, s, pt: (bb, 0, 0)),
            pl.BlockSpec((n_rows, PAGE_SIZE), lambda bb, s, pt: (0, 0)),
        ] + [page_spec(i) for i in range(pages)] + [page_spec(i) for i in range(pages)],
        out_specs=pl.BlockSpec((1, n_new, d), lambda bb, s, pt: (bb, 0, 0)),
        scratch_shapes=[pltpu.VMEM((n_rows, d), F32), pltpu.VMEM((n_rows, LANES), F32)],
    )
    return pl.pallas_call(
        _attn_sample_body,
        grid_spec=grid_spec,
        out_shape=jax.ShapeDtypeStruct((b, n_new, d), BF16),
        compiler_params=_cparams("parallel", "arbitrary"),
        name="sb_attn_sample",
    )(page_table, qbd, pad(k_new), pad(v_new), bias_rows,
      *([cache_k] * pages), *([cache_v] * pages))


def _rmsnorm_body(x_ref, g_ref, o_ref):
    x = x_ref[...]
    ms = jnp.mean(x * x, axis=-1, keepdims=True)
    o_ref[...] = x * lax.rsqrt(ms + RMS_EPS) * g_ref[...]


def _rmsnorm(x, g, *, tm):
    t, d = x.shape
    return pl.pallas_call(
        _rmsnorm_body,
        grid=(t // tm,),
        in_specs=[pl.BlockSpec((tm, d), lambda i: (i, 0)), pl.BlockSpec((1, d), lambda i: (0, 0))],
        out_specs=pl.BlockSpec((tm, d), lambda i: (i, 0)),
        out_shape=jax.ShapeDtypeStruct((t, d), F32),
        compiler_params=_cparams("parallel"),
        name="final_rmsnorm",
    )(x, g)


def _pad_lanes(x, n=LANES, value=0.0):
    return jnp.pad(x, [(0, 0)] * (x.ndim - 1) + [(0, n - x.shape[-1])], constant_values=value)


def kernel(x_prompt, x_sample, state_conv, state_ssm, cache_k, cache_v, page_table, g_mix, w_in, conv_w, conv_b, dt_bias, a_log, d_skip, g_ssm_norm, w_ssm_out, g_kv, w_k, w_v, g_attn, w_q, w_o, sb_bias, g_ffn, w_router, b_router, w_gu, b_gu, w_down, b_down, g_final):
    bp, lp, d = x_prompt.shape
    bs, ls, _ = x_sample.shape
    tp, ts = bp * lp, bs * ls
    streams = [x_prompt.reshape(tp, d), x_sample.reshape(ts, d)]
    tms = [1024, ts]

    w_in0 = w_in[0] * g_mix[0][:, None]
    w_zx = w_in0[:, :D_INNER + CONV_DIM].astype(BF16)
    w_dt = _pad_lanes(w_in0[:, D_INNER + CONV_DIM:])
    w_out = w_ssm_out[0].astype(BF16)
    w_q1 = (w_q[0] * g_attn[0][:, None] * (HEAD_DIM ** -0.5 * LOG2E)).astype(BF16)
    bias2 = sb_bias[0] * LOG2E
    w_k1 = (w_k * g_kv[:, None]).astype(BF16)
    w_v1 = (w_v * g_kv[:, None]).astype(BF16)
    w_o1 = w_o[0].astype(BF16)
    w_gu_b = w_gu.astype(BF16)
    w_down_b = w_down.astype(BF16)
    wr = _pad_lanes(w_router)
    br = _pad_lanes(b_router, value=NEG_BIG)[:, None, :]
    cw = jnp.pad(conv_w[0], ((0, SUBLANES - CONV_W), (0, 0)))
    expand_mat = (jnp.arange(D_INNER)[None, :] // SSM_HEAD_DIM
                  == jnp.arange(LANES)[:, None]).astype(BF16)
    dskip_x = jnp.repeat(d_skip[0], SSM_HEAD_DIM)[None, :]
    conv_pad = lambda cs: jnp.pad(cs, ((0, 0), (SUBLANES - (CONV_W - 1), 0), (0, 0)))
    conv0 = [jnp.zeros((bp, SUBLANES, CONV_DIM), F32), conv_pad(state_conv[0])]
    ssm0 = [jnp.zeros((bp, D_INNER, D_STATE), F32), state_ssm[0].reshape(bs, D_INNER, D_STATE)]

    new_conv, new_ssm, h1 = [], [], []
    for i, (h, tm) in enumerate(zip(streams, tms)):
        n_seq, l = (bp, lp) if i == 0 else (bs, ls)
        zx, = _norm_matmul(h, w_zx, norm=True, tm=tm, tn=1024, name="in_proj")
        dt_raw, = _norm_matmul(h, w_dt, norm=True, tm=tm, tn=LANES, precision=HIGHEST,
                               name="in_proj_dt")
        if l % SSD_CHUNK:
            assert l < SSD_CHUNK and l % SUBLANES == 0
            padrows = lambda x: jnp.pad(x.reshape(n_seq, l, -1), ((0, 0), (0, SSD_CHUNK - l), (0, 0))
                                        ).reshape(n_seq * SSD_CHUNK, -1)
            zx, dt_raw = padrows(zx), padrows(dt_raw)
            valid = l
        else:
            valid = SSD_CHUNK
        y, conv_o, ssm_o = _ssd(zx, dt_raw, conv0[i], ssm0[i], cw, conv_b[0][None, :],
                                _pad_lanes(dt_bias[0])[None, :], _pad_lanes(a_log[0])[None, :],
                                dskip_x, g_ssm_norm[0][None, :], expand_mat,
                                n_seq=n_seq, valid_len=valid)
        if l % SSD_CHUNK:
            y = y.reshape(n_seq, SSD_CHUNK, D_INNER)[:, :l].reshape(n_seq * l, D_INNER)
        new_conv.append(conv_o[:, SUBLANES - (CONV_W - 1):][None])
        new_ssm.append(ssm_o.reshape(n_seq, N_SSM_HEADS, SSM_HEAD_DIM, D_STATE)[None])
        hm, = _norm_matmul(y, w_out, norm=False, res=h, tm=tm, tn=1024, name="ssm_out_proj")
        h1.append(hm)
    h1 = _moe(h1, g_ffn[0][None, :], wr[0], br[0], w_gu_b[0], b_gu[0][:, None, :],
              w_down_b[0], b_down[0][:, None, :])

    qs, ks, vs, kbs, vbs = [], [], [], [], []
    for h, tm in zip(h1, tms):
        q, = _norm_matmul(h, w_q1, norm=True, out_dtypes=(BF16,), tm=tm, tn=1024, name="q_proj")
        k, kb = _norm_matmul(h, w_k1, norm=True, out_dtypes=(F32, BF16), tm=tm, tn=1024,
                             name="k_proj")
        v, vb = _norm_matmul(h, w_v1, norm=True, out_dtypes=(F32, BF16), tm=tm, tn=1024,
                             name="v_proj")
        qs.append(q); ks.append(k); vs.append(v); kbs.append(kb); vbs.append(vb)
    o_p = _attn_prompt(qs[0], kbs[0], vbs[0], bias2, n_seq=bp)
    n_phys = cache_k.shape[0]
    o_s = _attn_sample(qs[1].reshape(bs, ls, d), kbs[1].reshape(bs, ls, d), vbs[1].reshape(bs, ls, d),
                       bias2, cache_k.reshape(n_phys, PAGE_SIZE, d).astype(BF16),
                       cache_v.reshape(n_phys, PAGE_SIZE, d).astype(BF16), page_table).reshape(ts, d)
    h2 = []
    for h, o, tm in zip(h1, [o_p, o_s], tms):
        hm, = _norm_matmul(o, w_o1, norm=False, res=h, tm=tm, tn=1024, name="attn_out_proj")
        h2.append(hm)
    h2 = _moe(h2, g_ffn[1][None, :], wr[1], br[1], w_gu_b[1], b_gu[1][:, None, :],
              w_down_b[1], b_down[1][:, None, :])

    y_p = _rmsnorm(h2[0], g_final[None, :], tm=1024).reshape(bp, lp, d)
    y_s = _rmsnorm(h2[1], g_final[None, :], tm=ts).reshape(bs, ls, d)
    kv_shape = lambda x, b, l: x.reshape(b, l, N_HEADS, HEAD_DIM)
    return (y_p, y_s, new_conv[0], new_ssm[0], kv_shape(ks[0], bp, lp), kv_shape(vs[0], bp, lp),
            new_conv[1], new_ssm[1], kv_shape(ks[1], bs, ls), kv_shape(vs[1], bs, ls))
```

```python
import functools
import math

import jax
import jax.numpy as jnp
from jax import lax
from jax.experimental import pallas as pl
from jax.experimental.pallas import tpu as pltpu

F32 = jnp.float32
BF16 = jnp.bfloat16
HIGHEST = lax.Precision.HIGHEST

LANES = 128
SUBLANES = 8
VMEM_LIMIT = 48 * 1024 * 1024

D_MODEL = 1024
D_INNER = 2048
SSM_HEAD_DIM = 64
N_SSM_HEADS = 32
D_STATE = 128
N_GROUPS = 8
GROUP_W = D_INNER // N_GROUPS
CONV_W = 4
CONV_DIM = D_INNER + 2 * N_GROUPS * D_STATE
SSD_CHUNK = 128
HEAD_DIM = 64
N_HEADS = 16
N_EXPERTS = 32
TOP_K = 4
D_EXPERT = 1024
SWIGLU_ALPHA = 1.702
SWIGLU_LIMIT = 7.0
RMS_EPS = 1e-5
PAGE_SIZE = 128

MOE_BM = 256
ATTN_TQ = 256
ATTN_TK = 256
ATTN_UNROLL = 8
PAGES_PER_STEP = 4
NEG_BIG = -1e30


def _cparams(*sem, flags=None):
    return pltpu.CompilerParams(dimension_semantics=sem, vmem_limit_bytes=VMEM_LIMIT, flags=flags)


def _softplus(z):
    return jnp.maximum(z, 0.0) + jnp.log1p(jnp.exp(-jnp.abs(z)))


def _split_bf16(x):
    hi = x.astype(BF16)
    lo = (x - hi.astype(F32)).astype(BF16)
    return hi, lo


def _nm_body(*refs, norm, has_res, n_out, precision):
    x_ref, w_ref = refs[0], refs[1]
    pos = 2
    res_ref = None
    if has_res:
        res_ref = refs[pos]
        pos += 1
    out_refs = refs[pos:pos + n_out]
    pos += n_out
    if norm:
        xn_ref = refs[pos]

        @pl.when(pl.program_id(1) == 0)
        def _():
            x = x_ref[...].astype(F32)
            ms = jnp.mean(x * x, axis=-1, keepdims=True)
            xn_ref[...] = (x * lax.rsqrt(ms + RMS_EPS)).astype(xn_ref.dtype)

        lhs = xn_ref[...]
    else:
        lhs = x_ref[...]
    acc = jnp.dot(lhs, w_ref[...], preferred_element_type=F32, precision=precision)
    if has_res:
        acc = acc + res_ref[...]
    for o in out_refs:
        o[...] = acc.astype(o.dtype).reshape(o.shape)


def _norm_matmul(x, w, *, norm, res=None, out_dtypes=(F32,), tm, tn, precision=None,
                 split_heads=False, name="matmul"):
    t, k = x.shape
    n = w.shape[1]
    assert t % tm == 0 and n % tn == 0
    in_specs = [pl.BlockSpec((tm, k), lambda i, j: (i, 0)),
                pl.BlockSpec((k, tn), lambda i, j: (0, j))]
    args = [x, w]
    if res is not None:
        in_specs.append(pl.BlockSpec((tm, tn), lambda i, j: (i, j)))
        args.append(res)
    out_specs = [pl.BlockSpec((tm, tn), lambda i, j: (i, j)) for _ in out_dtypes]
    out_shape = [jax.ShapeDtypeStruct((t, n), d) for d in out_dtypes]
    if split_heads:
        assert n == tn == N_HEADS * HEAD_DIM
        out_specs[0] = pl.BlockSpec((tm, N_HEADS, HEAD_DIM), lambda i, j: (i, 0, 0))
        out_shape[0] = jax.ShapeDtypeStruct((t, N_HEADS, HEAD_DIM), out_dtypes[0])
    outs = pl.pallas_call(
        functools.partial(_nm_body, norm=norm, has_res=res is not None,
                          n_out=len(out_dtypes), precision=precision),
        grid=(t // tm, n // tn),
        in_specs=in_specs,
        out_specs=out_specs,
        out_shape=out_shape,
        scratch_shapes=[pltpu.VMEM((tm, k), w.dtype)] if norm else [],
        compiler_params=_cparams("parallel", "arbitrary"),
        name=name,
    )(*args)
    return outs


def _ssd_body(zx_ref, dt_ref, conv0_ref, ssm0_ref, cw_ref, cb_ref, dtb_ref, alog_ref,
              dskip_ref, gn_ref, e_ref,
              y_ref, conv_out_ref, ssm_ref,
              ext_ref, xbc_ref, xg_ref, xd_ref, *, valid_len):
    L = SSD_CHUNK
    c = pl.program_id(1)

    @pl.when(c == 0)
    def _():
        ext_ref[0:SUBLANES, :] = conv0_ref[0]
        ssm_ref[0] = ssm0_ref[0]

    ext_ref[SUBLANES:SUBLANES + L, :] = zx_ref[:, D_INNER:D_INNER + CONV_DIM]
    strip = 512
    for s in range(CONV_DIM // strip):
        cols = slice(s * strip, (s + 1) * strip)
        acc = jnp.broadcast_to(cb_ref[:, cols], (L, strip))
        for w in range(CONV_W):
            r0 = SUBLANES - (CONV_W - 1) + w
            acc = acc + ext_ref[r0:r0 + L, cols] * cw_ref[w:w + 1, cols]
        xbc_ref[:, cols] = acc * jax.nn.sigmoid(acc)
    conv_out_ref[0] = ext_ref[valid_len:valid_len + SUBLANES, :]
    ext_ref[0:SUBLANES, :] = ext_ref[L:L + SUBLANES, :]

    lane = lax.broadcasted_iota(jnp.int32, (L, LANES), 1)
    row = lax.broadcasted_iota(jnp.int32, (L, LANES), 0)
    live = (lane < N_SSM_HEADS) & (row < valid_len)
    dt = jnp.where(live, _softplus(dt_ref[...] + dtb_ref[...]), 0.0)
    a = dt * (-jnp.exp(alog_ref[...]))
    tril = (lax.broadcasted_iota(jnp.int32, (L, L), 0)
            >= lax.broadcasted_iota(jnp.int32, (L, L), 1))
    a_cum = jnp.dot(tril.astype(F32), a, preferred_element_type=F32, precision=HIGHEST)
    a_cum_t = a_cum.T
    a_last = a_cum[L - 1:L, :]
    ea = jnp.exp(a_cum)
    w_end = dt * jnp.exp(a_last - a_cum)
    chunk_decay_t = jnp.exp(a_cum_t[:, L - 1:L])

    def expand(p):
        hi, lo = _split_bf16(p)
        e = e_ref[...]
        return (jnp.dot(hi, e, preferred_element_type=F32)
                + jnp.dot(lo, e, preferred_element_type=F32))

    dt_x = expand(dt)
    wend_x = expand(w_end)
    ea_x = expand(ea)
    xs = xbc_ref[:, 0:D_INNER]
    xg_ref[...] = (xs * dt_x).astype(BF16)
    xd_ref[...] = xs * wend_x

    for g in range(N_GROUPS):
        gcols = slice(g * GROUP_W, (g + 1) * GROUP_W)
        b_g = xbc_ref[:, D_INNER + g * D_STATE:D_INNER + (g + 1) * D_STATE].astype(BF16)
        c_g = xbc_ref[:, D_INNER + N_GROUPS * D_STATE + g * D_STATE:
                      D_INNER + N_GROUPS * D_STATE + (g + 1) * D_STATE].astype(BF16)
        cb = lax.dot_general(c_g, b_g, (((1,), (1,)), ((), ())), preferred_element_type=F32)
        s_g = ssm_ref[0, g * GROUP_W:(g + 1) * GROUP_W, :]
        y_off = lax.dot_general(c_g, s_g.astype(BF16), (((1,), (1,)), ((), ())),
                                preferred_element_type=F32) * ea_x[:, gcols]
        y_parts = []
        for hh in range(GROUP_W // SSM_HEAD_DIM):
            h = g * (GROUP_W // SSM_HEAD_DIM) + hh
            seg = a_cum[:, h:h + 1] - a_cum_t[h:h + 1, :]
            m = (cb * jnp.where(tril, jnp.exp(seg), 0.0)).astype(BF16)
            y_parts.append(jnp.dot(m, xg_ref[:, h * SSM_HEAD_DIM:(h + 1) * SSM_HEAD_DIM],
                                   preferred_element_type=F32))
        y = jnp.concatenate(y_parts, axis=1) + y_off
        y = y + xbc_ref[:, gcols] * dskip_ref[:, gcols]
        xd_t = xd_ref[:, gcols].T.astype(BF16)
        contrib = jnp.dot(xd_t, b_g, preferred_element_type=F32)
        for hh in range(GROUP_W // SSM_HEAD_DIM):
            h = g * (GROUP_W // SSM_HEAD_DIM) + hh
            rows = slice(hh * SSM_HEAD_DIM, (hh + 1) * SSM_HEAD_DIM)
            ssm_ref[0, g * GROUP_W + hh * SSM_HEAD_DIM:g * GROUP_W + (hh + 1) * SSM_HEAD_DIM, :] = (
                s_g[rows, :] * chunk_decay_t[h:h + 1, :] + contrib[rows, :])
        z = zx_ref[:, gcols]
        yz = y * (z * jax.nn.sigmoid(z))
        ms = jnp.mean(yz * yz, axis=-1, keepdims=True)
        y_ref[:, gcols] = (yz * lax.rsqrt(ms + RMS_EPS) * gn_ref[:, gcols]).astype(y_ref.dtype)


def _ssd(zx, dt_raw, conv0, ssm0, cw, cb, dtb, alog, dskip, gn, expand_mat, *, n_seq, valid_len):
    rows = zx.shape[0]
    cps = rows // n_seq // SSD_CHUNK
    width = zx.shape[1]
    const = lambda shape: pl.BlockSpec(shape, lambda b, c: (0,) * len(shape))
    return pl.pallas_call(
        functools.partial(_ssd_body, valid_len=valid_len),
        grid=(n_seq, cps),
        in_specs=[
            pl.BlockSpec((SSD_CHUNK, width), lambda b, c: (b * cps + c, 0)),
            pl.BlockSpec((SSD_CHUNK, LANES), lambda b, c: (b * cps + c, 0)),
            pl.BlockSpec((1, SUBLANES, CONV_DIM), lambda b, c: (b, 0, 0)),
            pl.BlockSpec((1, D_INNER, D_STATE), lambda b, c: (b, 0, 0)),
            const((SUBLANES, CONV_DIM)), const((1, CONV_DIM)), const((1, LANES)), const((1, LANES)),
            const((1, D_INNER)), const((1, D_INNER)), const((LANES, D_INNER)),
        ],
        out_specs=[
            pl.BlockSpec((SSD_CHUNK, D_INNER), lambda b, c: (b * cps + c, 0)),
            pl.BlockSpec((1, SUBLANES, CONV_DIM), lambda b, c: (b, 0, 0)),
            pl.BlockSpec((1, D_INNER, D_STATE), lambda b, c: (b, 0, 0)),
        ],
        out_shape=[
            jax.ShapeDtypeStruct((rows, D_INNER), BF16),
            jax.ShapeDtypeStruct((n_seq, SUBLANES, CONV_DIM), F32),
            jax.ShapeDtypeStruct((n_seq, D_INNER, D_STATE), F32),
        ],
        scratch_shapes=[
            pltpu.VMEM((SSD_CHUNK + SUBLANES, CONV_DIM), F32),
            pltpu.VMEM((SSD_CHUNK, CONV_DIM), F32),
            pltpu.VMEM((SSD_CHUNK, D_INNER), BF16),
            pltpu.VMEM((SSD_CHUNK, D_INNER), F32),
        ],
        compiler_params=_cparams("parallel", "arbitrary"),
        name="ssd_mixer",
    )(zx, dt_raw, conv0, ssm0, cw, cb, dtb, alog, dskip, gn, expand_mat)


def _router_body(x_ref, g_ref, wr_ref, br_ref, cnt0_ref,
                 xn_ref, idx_ref, gate_ref, pos_ref, cnt_ref, cnt_scr):
    tm = x_ref.shape[0]

    @pl.when(pl.program_id(0) == 0)
    def _():
        cnt_scr[...] = cnt0_ref[...]

    x = x_ref[...]
    ms = jnp.mean(x * x, axis=-1, keepdims=True)
    xh = x * lax.rsqrt(ms + RMS_EPS) * g_ref[...]
    xn_ref[...] = xh.astype(xn_ref.dtype)
    logits = jnp.dot(xh, wr_ref[...], preferred_element_type=F32, precision=HIGHEST) + br_ref[...]
    lane = lax.broadcasted_iota(jnp.int32, (tm, LANES), 1)
    lane_f = lane.astype(F32)
    work = logits
    vals, idxs, hots = [], [], []
    for _ in range(TOP_K):
        m = jnp.max(work, axis=-1, keepdims=True)
        am = jnp.min(jnp.where(work == m, lane_f, float(LANES)), axis=-1, keepdims=True)
        hot = lane_f == am
        vals.append(m)
        idxs.append(am)
        hots.append(hot)
        work = jnp.where(hot, -jnp.inf, work)
    es = [jnp.exp(v - vals[0]) for v in vals]
    den = es[0] + es[1] + es[2] + es[3]
    sel = (hots[0] | hots[1] | hots[2] | hots[3])
    sel_f = sel.astype(F32)
    before = (lax.broadcasted_iota(jnp.int32, (tm, tm), 0)
              > lax.broadcasted_iota(jnp.int32, (tm, tm), 1)).astype(BF16)
    rank = jnp.dot(before, sel_f.astype(BF16), preferred_element_type=F32) + cnt_scr[0:1, :]
    cnt_new = cnt_scr[0:1, :] + jnp.sum(sel_f, axis=0, keepdims=True)
    cnt_scr[...] = jnp.broadcast_to(cnt_new, cnt_scr.shape)
    cnt_ref[...] = jnp.broadcast_to(cnt_new, cnt_ref.shape)
    idx_o = jnp.zeros((tm, LANES), jnp.int32)
    gate_o = jnp.zeros((tm, LANES), F32)
    pos_o = jnp.zeros((tm, LANES), jnp.int32)
    for k in range(TOP_K):
        p_k = jnp.sum(jnp.where(hots[k], rank, 0.0), axis=-1, keepdims=True).astype(jnp.int32)
        idx_o = jnp.where(lane == k, idxs[k].astype(jnp.int32), idx_o)
        gate_o = jnp.where(lane == k, es[k] / den, gate_o)
        pos_o = jnp.where(lane == k, p_k, pos_o)
    idx_ref[...] = idx_o
    gate_ref[...] = gate_o
    pos_ref[...] = pos_o


def _router(x, g, wr, br, cnt0, *, tm):
    t = x.shape[0]
    row = lambda w, d: (pl.BlockSpec((tm, w), lambda i: (i, 0)), jax.ShapeDtypeStruct((t, w), d))
    outs = [row(D_MODEL, BF16), row(LANES, jnp.int32), row(LANES, F32), row(LANES, jnp.int32)]
    const = lambda shape: pl.BlockSpec(shape, lambda i: (0,) * len(shape))
    return pl.pallas_call(
        _router_body,
        grid=(t // tm,),
        in_specs=[pl.BlockSpec((tm, D_MODEL), lambda i: (i, 0)),
                  const((1, D_MODEL)), const((D_MODEL, LANES)), const((1, LANES)),
                  const((SUBLANES, LANES))],
        out_specs=[o[0] for o in outs] + [const((SUBLANES, LANES))],
        out_shape=[o[1] for o in outs] + [jax.ShapeDtypeStruct((SUBLANES, LANES), F32)],
        scratch_shapes=[pltpu.VMEM((SUBLANES, LANES), F32)],
        compiler_params=_cparams("arbitrary"),
        name="moe_router",
    )(x, g, wr, br, cnt0)


def _experts_body(be_ref, nused_ref, x_ref, wgu_ref, bgu_ref, wd_ref, bd_ref, o_ref,
                  wgu_bf, wd_bf):
    i = pl.program_id(0)
    live = i < nused_ref[0]

    @pl.when(live & ((i == 0) | (be_ref[i] != be_ref[jnp.maximum(i - 1, 0)])))
    def _():
        wgu_bf[...] = wgu_ref[0].astype(BF16)
        wd_bf[...] = wd_ref[0].astype(BF16)

    @pl.when(live)
    def _():
        gu = jnp.dot(x_ref[...], wgu_bf[...], preferred_element_type=F32) + bgu_ref[0]
        x_glu = jnp.minimum(gu[:, :D_EXPERT], SWIGLU_LIMIT)
        x_lin = jnp.clip(gu[:, D_EXPERT:], -SWIGLU_LIMIT, SWIGLU_LIMIT)
        act = (x_lin + 1.0) * (x_glu * jax.nn.sigmoid(SWIGLU_ALPHA * x_glu))
        y = jnp.dot(act.astype(BF16), wd_bf[...], preferred_element_type=F32) + bd_ref[0]
        o_ref[...] = y.reshape(o_ref.shape)

    @pl.when(jnp.logical_not(live))
    def _():
        o_ref[...] = jnp.zeros(o_ref.shape, o_ref.dtype)


def _experts(x_sorted, block_expert, n_used, w_gu, b_gu, w_down, b_down, *, layer):
    n_slots = x_sorted.shape[0]
    n_blocks = n_slots // MOE_BM
    grid_spec = pltpu.PrefetchScalarGridSpec(
        num_scalar_prefetch=2,
        grid=(n_blocks,),
        in_specs=[
            pl.BlockSpec((MOE_BM, D_MODEL), lambda i, be, nu: (i, 0)),
            pl.BlockSpec((None, 1, D_MODEL, 2 * D_EXPERT), lambda i, be, nu: (layer, be[i], 0, 0)),
            pl.BlockSpec((None, 1, 1, 2 * D_EXPERT), lambda i, be, nu: (layer, be[i], 0, 0)),
            pl.BlockSpec((None, 1, D_EXPERT, D_MODEL), lambda i, be, nu: (layer, be[i], 0, 0)),
            pl.BlockSpec((None, 1, 1, D_MODEL), lambda i, be, nu: (layer, be[i], 0, 0)),
        ],
        out_specs=pl.BlockSpec((MOE_BM, SUBLANES, LANES), lambda i, be, nu: (i, 0, 0)),
        scratch_shapes=[pltpu.VMEM((D_MODEL, 2 * D_EXPERT), BF16),
                        pltpu.VMEM((D_EXPERT, D_MODEL), BF16)],
    )
    return pl.pallas_call(
        _experts_body,
        grid_spec=grid_spec,
        out_shape=jax.ShapeDtypeStruct((n_slots, SUBLANES, LANES), F32),
        compiler_params=_cparams("arbitrary"),
        name="moe_experts",
    )(block_expert, n_used, x_sorted, w_gu, b_gu, w_down, b_down)


def _combine_body(slots_ref, h_ref, gate_ref, y_hbm, o_ref, buf, sem):
    tb = h_ref.shape[0]

    def row_copy(src_row, k, r):
        return pltpu.make_async_copy(y_hbm.at[src_row], buf.at[k * tb + r], sem)

    def issue(r, c):
        for k in range(TOP_K):
            row_copy(slots_ref[r * TOP_K + k], k, r).start()
        return c

    lax.fori_loop(0, tb, issue, 0)
    for k in range(TOP_K):
        pltpu.make_async_copy(y_hbm.at[pl.ds(0, tb)], buf.at[pl.ds(k * tb, tb)], sem).wait()
    rows = buf[...].reshape(TOP_K * tb, D_MODEL)
    g = gate_ref[...]
    acc = h_ref[...]
    for k in range(TOP_K):
        acc = acc + g[:, k:k + 1] * rows[k * tb:(k + 1) * tb, :]
    o_ref[...] = acc


def _combine(h, gate, slots_flat, y_slots, *, tb):
    t = h.shape[0]
    return pl.pallas_call(
        _combine_body,
        grid=(t // tb,),
        in_specs=[
            pl.BlockSpec((tb * TOP_K,), lambda i: (i,), memory_space=pltpu.SMEM),
            pl.BlockSpec((tb, D_MODEL), lambda i: (i, 0)),
            pl.BlockSpec((tb, LANES), lambda i: (i, 0)),
            pl.BlockSpec(memory_space=pl.ANY),
        ],
        out_specs=pl.BlockSpec((tb, D_MODEL), lambda i: (i, 0)),
        out_shape=jax.ShapeDtypeStruct((t, D_MODEL), F32),
        scratch_shapes=[pltpu.VMEM((TOP_K * tb, SUBLANES, LANES), F32),
                        pltpu.SemaphoreType.DMA(())],
        compiler_params=_cparams("arbitrary"),
        name="moe_combine",
    )(slots_flat, h, gate, y_slots)


def _moe(h_streams, g, wr, br, w_gu, b_gu, w_down, b_down, *, layer):
    cnt = jnp.zeros((SUBLANES, LANES), F32)
    routed = []
    for h in h_streams:
        xn, idx, gate, pos, cnt = _router(h, g, wr, br, cnt, tm=256)
        routed.append((xn, idx[:, :TOP_K], gate, pos[:, :TOP_K]))
    counts = cnt[0, :N_EXPERTS].astype(jnp.int32)
    padded = (counts + MOE_BM - 1) // MOE_BM * MOE_BM
    pad_end = jnp.cumsum(padded)
    pad_start = pad_end - padded
    t_all = sum(h.shape[0] for h in h_streams)
    n_blocks = -(-(t_all * TOP_K + N_EXPERTS * (MOE_BM - 1)) // MOE_BM)
    n_slots = n_blocks * MOE_BM
    n_used = (pad_end[-1] // MOE_BM).astype(jnp.int32)
    blk = jnp.minimum(jnp.arange(n_blocks, dtype=jnp.int32), n_used - 1) * MOE_BM
    block_expert = jnp.minimum(jnp.sum(pad_end[None, :] <= blk[:, None], axis=1),
                               N_EXPERTS - 1).astype(jnp.int32)
    slots = [pad_start[idx] + pos for (_, idx, _, pos) in routed]
    slots_flat = jnp.concatenate([s.reshape(-1) for s in slots])
    tok = jnp.arange(t_all * TOP_K, dtype=jnp.int32) // TOP_K
    slot_token = jnp.zeros((n_slots,), jnp.int32).at[slots_flat].set(tok, unique_indices=True)
    xn_all = jnp.concatenate([r[0] for r in routed], axis=0)
    x_sorted = xn_all.at[slot_token].get(mode='promise_in_bounds')
    y_slots = _experts(x_sorted, block_expert, n_used.reshape(1), w_gu, b_gu, w_down, b_down,
                       layer=layer)
    return [_combine(h, gate, s.reshape(-1), y_slots, tb=256)
            for h, s, (_, _, gate, _) in zip(h_streams, slots, routed)]


LOG2E = 1.0 / math.log(2.0)


def _softplus2(y):
    neg_abs = lax.bitcast_convert_type(
        lax.bitcast_convert_type(y, jnp.uint32) | jnp.uint32(0x80000000), F32)
    return jnp.maximum(y, 0.0) + jnp.log(1.0 + jnp.exp2(neg_abs)) * LOG2E


def _sb_block(y, v, r_in, suffix_mat, mask):
    sp = _softplus2(y)
    if mask is not None:
        sp = jnp.where(mask, sp, 0.0)
    s_incl = jnp.dot(sp.astype(BF16), suffix_mat, preferred_element_type=F32) + r_in
    w = jnp.exp2(y - s_incl)
    if mask is not None:
        w = jnp.where(mask, w, 0.0)
    return jnp.dot(w.astype(BF16), v, preferred_element_type=F32), s_incl[:, 0:1]


def _suffix_matrix(n):
    return (lax.broadcasted_iota(jnp.int32, (n, n), 0)
            >= lax.broadcasted_iota(jnp.int32, (n, n), 1)).astype(BF16)


def _attn_prompt_body(bias_ref, q_ref, k_ref, v_ref, o_ref, y_scr, sp_scr):
    hp = pl.program_id(1)
    qi = pl.program_id(2)
    tq, tk = ATTN_TQ, ATTN_TK
    q = q_ref[...]
    lane = lax.broadcasted_iota(jnp.int32, (tq, LANES), 1)
    first = lane < HEAD_DIM
    suffix = _suffix_matrix(tk)
    diag_mask = (lax.broadcasted_iota(jnp.int32, (tq, tk), 1)
                 < lax.broadcasted_iota(jnp.int32, (tq, tk), 0))
    heads = ((jnp.where(first, q, jnp.zeros_like(q)), bias_ref[2 * hp]),
             (jnp.where(first, jnp.zeros_like(q), q), bias_ref[2 * hp + 1]))

    def logits(j, masked):
        kb = k_ref[pl.ds(pl.multiple_of(j * tk, tk), tk), :]
        ys = []
        for qh, bias in heads:
            y = lax.dot_general(qh, kb, (((1,), (1,)), ((), ())),
                                preferred_element_type=F32) + bias
            if masked:
                y = jnp.where(diag_mask, y, NEG_BIG)
            ys.append(y)
        return ys

    def split_store(ys, slot):
        for h, y in enumerate(ys):
            y_scr[h, slot] = y
            sp_scr[h, slot] = _softplus2(y).astype(BF16)

    def suffix_sums(slot):
        return [jnp.dot(sp_scr[h, slot], suffix, preferred_element_type=F32) for h in range(2)]

    def finish(s2s, j, slot, carry):
        vb = v_ref[pl.ds(pl.multiple_of(j * tk, tk), tk), :]
        outs, rs = [], []
        for h, s2 in enumerate(s2s):
            s_incl = s2 + carry[h]
            w = jnp.exp2(y_scr[h, slot] - s_incl)
            outs.append(jnp.dot(w.astype(BF16), vb, preferred_element_type=F32))
            rs.append(s_incl[:, 0:1])
        return rs[0], rs[1], carry[2] + jnp.where(first, outs[0], outs[1])

    def step(j_next, j_cur, cur_slot, carry):
        ys = logits(j_next, False)
        s2s = suffix_sums(cur_slot)
        split_store(ys, 1 - cur_slot)
        return finish(s2s, j_cur, cur_slot, carry)

    def steps(j, n, carry):
        for i in range(n):
            carry = step(j - i - 1, j - i, i % 2, carry)
        return carry

    unroll = ATTN_UNROLL
    split_store(logits(qi, True), 0)
    carry = (jnp.zeros((tq, 1), F32), jnp.zeros((tq, 1), F32), jnp.zeros((tq, LANES), F32))
    carry = lax.fori_loop(0, qi // unroll, lambda u, c: steps(qi - unroll * u, unroll, c), carry)
    rest = qi % unroll
    carry = lax.fori_loop(0, rest // 2, lambda u, c: steps(rest - 2 * u, 2, c), carry)
    odd = qi % 2

    def tail(c):
        c = step(0, 1, 0, c)
        return finish(suffix_sums(1), 0, 1, c)

    def last(c):
        return finish(suffix_sums(0), 0, 0, c)

    carry = lax.cond(odd == 1, tail, last, carry)
    o_ref[...] = carry[2].astype(o_ref.dtype)


def _attn_prompt(q, k, v, bias, *, n_seq):
    t = q.shape[0]
    l = t // n_seq
    assert ATTN_TQ == ATTN_TK and l % ATTN_TQ == 0
    nq = l // ATTN_TQ
    grid_spec = pltpu.PrefetchScalarGridSpec(
        num_scalar_prefetch=0,
        grid=(n_seq, N_HEADS // 2, nq),
        in_specs=[
            pl.BlockSpec(memory_space=pltpu.SMEM),
            pl.BlockSpec((ATTN_TQ, LANES), lambda b, hp, qi: (b * nq + qi, hp)),
            pl.BlockSpec((l, LANES), lambda b, hp, qi: (b, hp)),
            pl.BlockSpec((l, LANES), lambda b, hp, qi: (b, hp)),
        ],
        out_specs=pl.BlockSpec((ATTN_TQ, LANES), lambda b, hp, qi: (b * nq + qi, hp)),
        scratch_shapes=[pltpu.VMEM((2, 2, ATTN_TQ, ATTN_TK), F32),
                        pltpu.VMEM((2, 2, ATTN_TQ, ATTN_TK), BF16)],
    )
    return pl.pallas_call(
        _attn_prompt_body,
        grid_spec=grid_spec,
        out_shape=jax.ShapeDtypeStruct((t, N_HEADS * HEAD_DIM), BF16),
        compiler_params=_cparams("parallel", "parallel", "arbitrary"),
        name="sb_attn_prompt",
    )(bias, q, k, v)


def _attn_sample_body(pt_ref, qbd_ref, kn_ref, vn_ref, bias_ref, *refs):
    pages = PAGES_PER_STEP
    k_refs = refs[:pages]
    v_refs = refs[pages:2 * pages]
    o_ref = refs[2 * pages]
    acc_ref, r_ref = refs[2 * pages + 1:]
    s = pl.program_id(1)
    n_rows = qbd_ref.shape[1]
    n_new = n_rows // N_HEADS
    qbd = qbd_ref[0]
    bias = bias_ref[...]
    suffix = _suffix_matrix(PAGE_SIZE)

    def logits(kb):
        return lax.dot_general(qbd, kb, (((1,), (1,)), ((), ())), preferred_element_type=F32) + bias

    @pl.when(s == 0)
    def _():
        col = lax.broadcasted_iota(jnp.int32, (n_rows, PAGE_SIZE), 1)
        t_of_row = lax.broadcasted_iota(jnp.int32, (n_rows, PAGE_SIZE), 0) % n_new
        contrib, r = _sb_block(logits(kn_ref[0]), vn_ref[0], jnp.zeros((n_rows, 1), F32),
                               suffix, col < t_of_row)
        acc_ref[...] = contrib
        r_ref[...] = jnp.broadcast_to(r, r_ref.shape)

    def page(ref):
        return ref[0].reshape(PAGE_SIZE, N_HEADS * HEAD_DIM).astype(BF16)

    for i in range(pages):
        contrib, r = _sb_block(logits(page(k_refs[i])), page(v_refs[i]), r_ref[:, 0:1],
                               suffix, None)
        acc_ref[...] += contrib
        r_ref[...] = jnp.broadcast_to(r, r_ref.shape)

    @pl.when(s == pl.num_programs(1) - 1)
    def _():
        lane_head = lax.broadcasted_iota(jnp.int32, (n_new, N_HEADS * HEAD_DIM), 1) // HEAD_DIM
        out = jnp.zeros((n_new, N_HEADS * HEAD_DIM), F32)
        for h in range(N_HEADS):
            out = jnp.where(lane_head == h, acc_ref[h * n_new:(h + 1) * n_new, :], out)
        o_ref[0] = out.astype(o_ref.dtype)


def _attn_sample(q, k_new, v_new, bias, cache_k, cache_v, page_table):
    b, n_new, d = q.shape
    n_pages = page_table.shape[1]
    pages = PAGES_PER_STEP
    assert n_pages % pages == 0 and n_new <= PAGE_SIZE
    n_rows = N_HEADS * n_new
    q4 = q.reshape(b, n_new, N_HEADS, HEAD_DIM).transpose(0, 2, 1, 3)
    eye = jnp.eye(N_HEADS, dtype=q.dtype)
    qbd = (q4[:, :, :, None, :] * eye[None, :, None, :, None]).reshape(b, n_rows, d)
    pad = lambda x: jnp.pad(x, ((0, 0), (0, PAGE_SIZE - n_new), (0, 0)))
    bias_rows = jnp.broadcast_to(jnp.repeat(bias, n_new)[:, None], (n_rows, PAGE_SIZE)).astype(F32)

    def page_spec(i):
        return pl.BlockSpec((1, PAGE_SIZE, N_HEADS, HEAD_DIM),
                            lambda bb, s, pt: (pt[bb, n_pages - 1 - (s * pages + i)], 0, 0, 0))

    grid_spec = pltpu.PrefetchScalarGridSpec(
        num_scalar_prefetch=1,
        grid=(b, n_pages // pages),
        in_specs=[
            pl.BlockSpec((1, n_rows, d), lambda bb, s, pt: (bb, 0, 0)),
            pl.BlockSpec((1, PAGE_SIZE, d), lambda bb, s, pt: (bb, 0, 0)),
            pl.BlockSpec((1, PAGE_SIZE, d), lambda bb, s, pt: (bb, 0, 0)),
            pl.BlockSpec((n_rows, PAGE_SIZE), lambda bb, s, pt: (0, 0)),
        ] + [page_spec(i) for i in range(pages)] + [page_spec(i) for i in range(pages)],
        out_specs=pl.BlockSpec((1, n_new, d), lambda bb, s, pt: (bb, 0, 0)),
        scratch_shapes=[pltpu.VMEM((n_rows, d), F32), pltpu.VMEM((n_rows, LANES), F32)],
    )
    return pl.pallas_call(
        _attn_sample_body,
        grid_spec=grid_spec,
        out_shape=jax.ShapeDtypeStruct((b, n_new, d), BF16),
        compiler_params=_cparams("parallel", "arbitrary"),
        name="sb_attn_sample",
    )(page_table, qbd, pad(k_new), pad(v_new), bias_rows,
      *([cache_k] * pages), *([cache_v] * pages))


def _rmsnorm_body(x_ref, g_ref, o_ref):
    x = x_ref[...]
    ms = jnp.mean(x * x, axis=-1, keepdims=True)
    o_ref[...] = x * lax.rsqrt(ms + RMS_EPS) * g_ref[...]


def _rmsnorm(x, g, *, tm):
    t, d = x.shape
    return pl.pallas_call(
        _rmsnorm_body,
        grid=(t // tm,),
        in_specs=[pl.BlockSpec((tm, d), lambda i: (i, 0)), pl.BlockSpec((1, d), lambda i: (0, 0))],
        out_specs=pl.BlockSpec((tm, d), lambda i: (i, 0)),
        out_shape=jax.ShapeDtypeStruct((t, d), F32),
        compiler_params=_cparams("parallel"),
        name="final_rmsnorm",
    )(x, g)


def _pad_lanes(x, n=LANES, value=0.0):
    return jnp.pad(x, [(0, 0)] * (x.ndim - 1) + [(0, n - x.shape[-1])], constant_values=value)


def kernel(x_prompt, x_sample, state_conv, state_ssm, cache_k, cache_v, page_table, g_mix, w_in, conv_w, conv_b, dt_bias, a_log, d_skip, g_ssm_norm, w_ssm_out, g_kv, w_k, w_v, g_attn, w_q, w_o, sb_bias, g_ffn, w_router, b_router, w_gu, b_gu, w_down, b_down, g_final):
    bp, lp, d = x_prompt.shape
    bs, ls, _ = x_sample.shape
    tp, ts = bp * lp, bs * ls
    streams = [x_prompt.reshape(tp, d), x_sample.reshape(ts, d)]
    tms = [1024, ts]

    w_in0 = w_in[0] * g_mix[0][:, None]
    w_zx = w_in0[:, :D_INNER + CONV_DIM].astype(BF16)
    w_dt = _pad_lanes(w_in0[:, D_INNER + CONV_DIM:])
    w_out = w_ssm_out[0].astype(BF16)
    w_q1 = (w_q[0] * g_attn[0][:, None] * (HEAD_DIM ** -0.5 * LOG2E)).astype(BF16)
    bias2 = sb_bias[0] * LOG2E
    w_k1 = (w_k * g_kv[:, None]).astype(BF16)
    w_v1 = (w_v * g_kv[:, None]).astype(BF16)
    w_o1 = w_o[0].astype(BF16)
    b_gu4 = b_gu[:, :, None, :]
    b_down4 = b_down[:, :, None, :]
    wr = _pad_lanes(w_router)
    br = _pad_lanes(b_router, value=NEG_BIG)[:, None, :]
    cw = jnp.pad(conv_w[0], ((0, SUBLANES - CONV_W), (0, 0)))
    expand_mat = (jnp.arange(D_INNER)[None, :] // SSM_HEAD_DIM
                  == jnp.arange(LANES)[:, None]).astype(BF16)
    dskip_x = jnp.repeat(d_skip[0], SSM_HEAD_DIM)[None, :]
    conv_pad = lambda cs: jnp.pad(cs, ((0, 0), (SUBLANES - (CONV_W - 1), 0), (0, 0)))
    conv0 = [jnp.zeros((bp, SUBLANES, CONV_DIM), F32), conv_pad(state_conv[0])]
    ssm0 = [jnp.zeros((bp, D_INNER, D_STATE), F32), state_ssm[0].reshape(bs, D_INNER, D_STATE)]

    new_conv, new_ssm, h1 = [], [], []
    for i, (h, tm) in enumerate(zip(streams, tms)):
        n_seq, l = (bp, lp) if i == 0 else (bs, ls)
        zx, = _norm_matmul(h, w_zx, norm=True, tm=tm, tn=1024, name="in_proj")
        dt_raw, = _norm_matmul(h, w_dt, norm=True, tm=tm, tn=LANES, precision=HIGHEST,
                               name="in_proj_dt")
        if l % SSD_CHUNK:
            assert l < SSD_CHUNK and l % SUBLANES == 0
            padrows = lambda x: jnp.pad(x.reshape(n_seq, l, -1), ((0, 0), (0, SSD_CHUNK - l), (0, 0))
                                        ).reshape(n_seq * SSD_CHUNK, -1)
            zx, dt_raw = padrows(zx), padrows(dt_raw)
            valid = l
        else:
            valid = SSD_CHUNK
        y, conv_o, ssm_o = _ssd(zx, dt_raw, conv0[i], ssm0[i], cw, conv_b[0][None, :],
                                _pad_lanes(dt_bias[0])[None, :], _pad_lanes(a_log[0])[None, :],
                                dskip_x, g_ssm_norm[0][None, :], expand_mat,
                                n_seq=n_seq, valid_len=valid)
        if l % SSD_CHUNK:
            y = y.reshape(n_seq, SSD_CHUNK, D_INNER)[:, :l].reshape(n_seq * l, D_INNER)
        new_conv.append(conv_o[:, SUBLANES - (CONV_W - 1):][None])
        new_ssm.append(ssm_o.reshape(n_seq, N_SSM_HEADS, SSM_HEAD_DIM, D_STATE)[None])
        hm, = _norm_matmul(y, w_out, norm=False, res=h, tm=tm, tn=1024, name="ssm_out_proj")
        h1.append(hm)
    h1 = _moe(h1, g_ffn[0][None, :], wr[0], br[0], w_gu, b_gu4, w_down, b_down4, layer=0)

    qs, ks, vs, kbs, vbs = [], [], [], [], []
    for h, tm in zip(h1, tms):
        q, = _norm_matmul(h, w_q1, norm=True, out_dtypes=(BF16,), tm=tm, tn=1024, name="q_proj")
        k, kb = _norm_matmul(h, w_k1, norm=True, out_dtypes=(F32, BF16), tm=min(tm, 512), tn=1024,
                             split_heads=True, name="k_proj")
        v, vb = _norm_matmul(h, w_v1, norm=True, out_dtypes=(F32, BF16), tm=min(tm, 512), tn=1024,
                             split_heads=True, name="v_proj")
        qs.append(q); ks.append(k); vs.append(v); kbs.append(kb); vbs.append(vb)
    o_p = _attn_prompt(qs[0], kbs[0], vbs[0], bias2, n_seq=bp)
    n_phys = cache_k.shape[0]
    o_s = _attn_sample(qs[1].reshape(bs, ls, d), kbs[1].reshape(bs, ls, d), vbs[1].reshape(bs, ls, d),
                       bias2, cache_k, cache_v, page_table).reshape(ts, d)
    h2 = []
    for h, o, tm in zip(h1, [o_p, o_s], tms):
        hm, = _norm_matmul(o, w_o1, norm=False, res=h, tm=tm, tn=1024, name="attn_out_proj")
        h2.append(hm)
    h2 = _moe(h2, g_ffn[1][None, :], wr[1], br[1], w_gu, b_gu4, w_down, b_down4, layer=1)

    y_p = _rmsnorm(h2[0], g_final[None, :], tm=1024).reshape(bp, lp, d)
    y_s = _rmsnorm(h2[1], g_final[None, :], tm=ts).reshape(bs, ls, d)
    kv_shape = lambda x, b, l: x.reshape(b, l, N_HEADS, HEAD_DIM)
    return (y_p, y_s, new_conv[0], new_ssm[0], kv_shape(ks[0], bp, lp), kv_shape(vs[0], bp, lp),
            new_conv[1], new_ssm[1], kv_shape(ks[1], bs, ls), kv_shape(vs[1], bs, ls))
```

```python
import functools
import math

import jax
import jax.numpy as jnp
from jax import lax
from jax.experimental import pallas as pl
from jax.experimental.pallas import tpu as pltpu

F32 = jnp.float32
BF16 = jnp.bfloat16
HIGHEST = lax.Precision.HIGHEST

LANES = 128
SUBLANES = 8
VMEM_LIMIT = 48 * 1024 * 1024

D_MODEL = 1024
D_INNER = 2048
SSM_HEAD_DIM = 64
N_SSM_HEADS = 32
D_STATE = 128
N_GROUPS = 8
GROUP_W = D_INNER // N_GROUPS
CONV_W = 4
CONV_DIM = D_INNER + 2 * N_GROUPS * D_STATE
SSD_CHUNK = 128
HEAD_DIM = 64
N_HEADS = 16
N_EXPERTS = 32
TOP_K = 4
D_EXPERT = 1024
SWIGLU_ALPHA = 1.702
SWIGLU_LIMIT = 7.0
RMS_EPS = 1e-5
PAGE_SIZE = 128

MOE_BM = 256
ATTN_TQ = 256
ATTN_TK = 256
ATTN_UNROLL = 8
PAGES_PER_STEP = 4
NEG_BIG = -1e30


def _cparams(*sem, flags=None):
    return pltpu.CompilerParams(dimension_semantics=sem, vmem_limit_bytes=VMEM_LIMIT, flags=flags)


def _softplus(z):
    return jnp.maximum(z, 0.0) + jnp.log1p(jnp.exp(-jnp.abs(z)))


def _split_bf16(x):
    hi = x.astype(BF16)
    lo = (x - hi.astype(F32)).astype(BF16)
    return hi, lo


def _nm_body(*refs, norm, has_res, n_out, precision):
    x_ref, w_ref = refs[0], refs[1]
    pos = 2
    res_ref = None
    if has_res:
        res_ref = refs[pos]
        pos += 1
    out_refs = refs[pos:pos + n_out]
    pos += n_out
    if norm:
        xn_ref = refs[pos]

        @pl.when(pl.program_id(1) == 0)
        def _():
            x = x_ref[...].astype(F32)
            ms = jnp.mean(x * x, axis=-1, keepdims=True)
            xn_ref[...] = (x * lax.rsqrt(ms + RMS_EPS)).astype(xn_ref.dtype)

        lhs = xn_ref[...]
    else:
        lhs = x_ref[...]
    acc = jnp.dot(lhs, w_ref[...], preferred_element_type=F32, precision=precision)
    if has_res:
        acc = acc + res_ref[...]
    for o in out_refs:
        o[...] = acc.astype(o.dtype).reshape(o.shape)


def _norm_matmul(x, w, *, norm, res=None, out_dtypes=(F32,), tm, tn, precision=None,
                 split_heads=False, name="matmul"):
    t, k = x.shape
    n = w.shape[1]
    assert t % tm == 0 and n % tn == 0
    in_specs = [pl.BlockSpec((tm, k), lambda i, j: (i, 0)),
                pl.BlockSpec((k, tn), lambda i, j: (0, j))]
    args = [x, w]
    if res is not None:
        in_specs.append(pl.BlockSpec((tm, tn), lambda i, j: (i, j)))
        args.append(res)
    out_specs = [pl.BlockSpec((tm, tn), lambda i, j: (i, j)) for _ in out_dtypes]
    out_shape = [jax.ShapeDtypeStruct((t, n), d) for d in out_dtypes]
    if split_heads:
        assert n == tn == N_HEADS * HEAD_DIM
        out_specs[0] = pl.BlockSpec((tm, N_HEADS, HEAD_DIM), lambda i, j: (i, 0, 0))
        out_shape[0] = jax.ShapeDtypeStruct((t, N_HEADS, HEAD_DIM), out_dtypes[0])
    outs = pl.pallas_call(
        functools.partial(_nm_body, norm=norm, has_res=res is not None,
                          n_out=len(out_dtypes), precision=precision),
        grid=(t // tm, n // tn),
        in_specs=in_specs,
        out_specs=out_specs,
        out_shape=out_shape,
        scratch_shapes=[pltpu.VMEM((tm, k), w.dtype)] if norm else [],
        compiler_params=_cparams("parallel", "arbitrary"),
        name=name,
    )(*args)
    return outs


def _ssd_body(zx_ref, dt_ref, conv0_ref, ssm0_ref, cw_ref, cb_ref, dtb_ref, alog_ref,
              dskip_ref, gn_ref, e_ref,
              y_ref, conv_out_ref, ssm_ref,
              ext_ref, xbc_ref, xg_ref, xd_ref, *, valid_len):
    L = SSD_CHUNK
    c = pl.program_id(1)

    @pl.when(c == 0)
    def _():
        ext_ref[0:SUBLANES, :] = conv0_ref[0]
        ssm_ref[0] = ssm0_ref[0]

    ext_ref[SUBLANES:SUBLANES + L, :] = zx_ref[:, D_INNER:D_INNER + CONV_DIM]
    strip = 512
    for s in range(CONV_DIM // strip):
        cols = slice(s * strip, (s + 1) * strip)
        acc = jnp.broadcast_to(cb_ref[:, cols], (L, strip))
        for w in range(CONV_W):
            r0 = SUBLANES - (CONV_W - 1) + w
            acc = acc + ext_ref[r0:r0 + L, cols] * cw_ref[w:w + 1, cols]
        xbc_ref[:, cols] = acc * jax.nn.sigmoid(acc)
    conv_out_ref[0] = ext_ref[valid_len:valid_len + SUBLANES, :]
    ext_ref[0:SUBLANES, :] = ext_ref[L:L + SUBLANES, :]

    lane = lax.broadcasted_iota(jnp.int32, (L, LANES), 1)
    row = lax.broadcasted_iota(jnp.int32, (L, LANES), 0)
    live = (lane < N_SSM_HEADS) & (row < valid_len)
    dt = jnp.where(live, _softplus(dt_ref[...] + dtb_ref[...]), 0.0)
    a = dt * (-jnp.exp(alog_ref[...]))
    tril = (lax.broadcasted_iota(jnp.int32, (L, L), 0)
            >= lax.broadcasted_iota(jnp.int32, (L, L), 1))
    a_cum = jnp.dot(tril.astype(F32), a, preferred_element_type=F32, precision=HIGHEST)
    a_cum_t = a_cum.T
    a_last = a_cum[L - 1:L, :]
    ea = jnp.exp(a_cum)
    w_end = dt * jnp.exp(a_last - a_cum)
    chunk_decay_t = jnp.exp(a_cum_t[:, L - 1:L])

    def expand(p):
        hi, lo = _split_bf16(p)
        e = e_ref[...]
        return (jnp.dot(hi, e, preferred_element_type=F32)
                + jnp.dot(lo, e, preferred_element_type=F32))

    dt_x = expand(dt)
    wend_x = expand(w_end)
    ea_x = expand(ea)
    xs = xbc_ref[:, 0:D_INNER]
    xg_ref[...] = (xs * dt_x).astype(BF16)
    xd_ref[...] = xs * wend_x

    for g in range(N_GROUPS):
        gcols = slice(g * GROUP_W, (g + 1) * GROUP_W)
        b_g = xbc_ref[:, D_INNER + g * D_STATE:D_INNER + (g + 1) * D_STATE].astype(BF16)
        c_g = xbc_ref[:, D_INNER + N_GROUPS * D_STATE + g * D_STATE:
                      D_INNER + N_GROUPS * D_STATE + (g + 1) * D_STATE].astype(BF16)
        cb = lax.dot_general(c_g, b_g, (((1,), (1,)), ((), ())), preferred_element_type=F32)
        s_g = ssm_ref[0, g * GROUP_W:(g + 1) * GROUP_W, :]
        y_off = lax.dot_general(c_g, s_g.astype(BF16), (((1,), (1,)), ((), ())),
                                preferred_element_type=F32) * ea_x[:, gcols]
        y_parts = []
        for hh in range(GROUP_W // SSM_HEAD_DIM):
            h = g * (GROUP_W // SSM_HEAD_DIM) + hh
            seg = a_cum[:, h:h + 1] - a_cum_t[h:h + 1, :]
            m = (cb * jnp.where(tril, jnp.exp(seg), 0.0)).astype(BF16)
            y_parts.append(jnp.dot(m, xg_ref[:, h * SSM_HEAD_DIM:(h + 1) * SSM_HEAD_DIM],
                                   preferred_element_type=F32))
        y = jnp.concatenate(y_parts, axis=1) + y_off
        y = y + xbc_ref[:, gcols] * dskip_ref[:, gcols]
        xd_t = xd_ref[:, gcols].T.astype(BF16)
        contrib = jnp.dot(xd_t, b_g, preferred_element_type=F32)
        for hh in range(GROUP_W // SSM_HEAD_DIM):
            h = g * (GROUP_W // SSM_HEAD_DIM) + hh
            rows = slice(hh * SSM_HEAD_DIM, (hh + 1) * SSM_HEAD_DIM)
            ssm_ref[0, g * GROUP_W + hh * SSM_HEAD_DIM:g * GROUP_W + (hh + 1) * SSM_HEAD_DIM, :] = (
                s_g[rows, :] * chunk_decay_t[h:h + 1, :] + contrib[rows, :])
        z = zx_ref[:, gcols]
        yz = y * (z * jax.nn.sigmoid(z))
        ms = jnp.mean(yz * yz, axis=-1, keepdims=True)
        y_ref[:, gcols] = (yz * lax.rsqrt(ms + RMS_EPS) * gn_ref[:, gcols]).astype(y_ref.dtype)


def _ssd(zx, dt_raw, conv0, ssm0, cw, cb, dtb, alog, dskip, gn, expand_mat, *, n_seq, valid_len):
    rows = zx.shape[0]
    cps = rows // n_seq // SSD_CHUNK
    width = zx.shape[1]
    const = lambda shape: pl.BlockSpec(shape, lambda b, c: (0,) * len(shape))
    return pl.pallas_call(
        functools.partial(_ssd_body, valid_len=valid_len),
        grid=(n_seq, cps),
        in_specs=[
            pl.BlockSpec((SSD_CHUNK, width), lambda b, c: (b * cps + c, 0)),
            pl.BlockSpec((SSD_CHUNK, LANES), lambda b, c: (b * cps + c, 0)),
            pl.BlockSpec((1, SUBLANES, CONV_DIM), lambda b, c: (b, 0, 0)),
            pl.BlockSpec((1, D_INNER, D_STATE), lambda b, c: (b, 0, 0)),
            const((SUBLANES, CONV_DIM)), const((1, CONV_DIM)), const((1, LANES)), const((1, LANES)),
            const((1, D_INNER)), const((1, D_INNER)), const((LANES, D_INNER)),
        ],
        out_specs=[
            pl.BlockSpec((SSD_CHUNK, D_INNER), lambda b, c: (b * cps + c, 0)),
            pl.BlockSpec((1, SUBLANES, CONV_DIM), lambda b, c: (b, 0, 0)),
            pl.BlockSpec((1, D_INNER, D_STATE), lambda b, c: (b, 0, 0)),
        ],
        out_shape=[
            jax.ShapeDtypeStruct((rows, D_INNER), BF16),
            jax.ShapeDtypeStruct((n_seq, SUBLANES, CONV_DIM), F32),
            jax.ShapeDtypeStruct((n_seq, D_INNER, D_STATE), F32),
        ],
        scratch_shapes=[
            pltpu.VMEM((SSD_CHUNK + SUBLANES, CONV_DIM), F32),
            pltpu.VMEM((SSD_CHUNK, CONV_DIM), F32),
            pltpu.VMEM((SSD_CHUNK, D_INNER), BF16),
            pltpu.VMEM((SSD_CHUNK, D_INNER), F32),
        ],
        compiler_params=_cparams("parallel", "arbitrary"),
        name="ssd_mixer",
    )(zx, dt_raw, conv0, ssm0, cw, cb, dtb, alog, dskip, gn, expand_mat)


def _router_body(x_ref, g_ref, wr_ref, br_ref, cnt0_ref,
                 xn_ref, idx_ref, gate_ref, pos_ref, cnt_ref, cnt_scr):
    tm = x_ref.shape[0]

    @pl.when(pl.program_id(0) == 0)
    def _():
        cnt_scr[...] = cnt0_ref[...]

    x = x_ref[...]
    ms = jnp.mean(x * x, axis=-1, keepdims=True)
    xh = x * lax.rsqrt(ms + RMS_EPS) * g_ref[...]
    xn_ref[...] = xh.reshape(xn_ref.shape)
    logits = jnp.dot(xh, wr_ref[...], preferred_element_type=F32, precision=HIGHEST) + br_ref[...]
    lane = lax.broadcasted_iota(jnp.int32, (tm, LANES), 1)
    lane_f = lane.astype(F32)
    work = logits
    vals, idxs, hots = [], [], []
    for _ in range(TOP_K):
        m = jnp.max(work, axis=-1, keepdims=True)
        am = jnp.min(jnp.where(work == m, lane_f, float(LANES)), axis=-1, keepdims=True)
        hot = lane_f == am
        vals.append(m)
        idxs.append(am)
        hots.append(hot)
        work = jnp.where(hot, -jnp.inf, work)
    es = [jnp.exp(v - vals[0]) for v in vals]
    den = es[0] + es[1] + es[2] + es[3]
    sel = (hots[0] | hots[1] | hots[2] | hots[3])
    sel_f = sel.astype(F32)
    before = (lax.broadcasted_iota(jnp.int32, (tm, tm), 0)
              > lax.broadcasted_iota(jnp.int32, (tm, tm), 1)).astype(BF16)
    rank = jnp.dot(before, sel_f.astype(BF16), preferred_element_type=F32) + cnt_scr[0:1, :]
    cnt_new = cnt_scr[0:1, :] + jnp.sum(sel_f, axis=0, keepdims=True)
    cnt_scr[...] = jnp.broadcast_to(cnt_new, cnt_scr.shape)
    cnt_ref[...] = jnp.broadcast_to(cnt_new, cnt_ref.shape)
    idx_o = jnp.zeros((tm, LANES), jnp.int32)
    gate_o = jnp.zeros((tm, LANES), F32)
    pos_o = jnp.zeros((tm, LANES), jnp.int32)
    for k in range(TOP_K):
        p_k = jnp.sum(jnp.where(hots[k], rank, 0.0), axis=-1, keepdims=True).astype(jnp.int32)
        idx_o = jnp.where(lane == k, idxs[k].astype(jnp.int32), idx_o)
        gate_o = jnp.where(lane == k, es[k] / den, gate_o)
        pos_o = jnp.where(lane == k, p_k, pos_o)
    idx_ref[...] = idx_o
    gate_ref[...] = gate_o
    pos_ref[...] = pos_o


def _router(x, g, wr, br, cnt0, *, tm):
    t = x.shape[0]
    row = lambda w, d: (pl.BlockSpec((tm, w), lambda i: (i, 0)), jax.ShapeDtypeStruct((t, w), d))
    tiles = (pl.BlockSpec((tm, SUBLANES, LANES), lambda i: (i, 0, 0)),
             jax.ShapeDtypeStruct((t, SUBLANES, LANES), F32))
    outs = [tiles, row(LANES, jnp.int32), row(LANES, F32), row(LANES, jnp.int32)]
    const = lambda shape: pl.BlockSpec(shape, lambda i: (0,) * len(shape))
    return pl.pallas_call(
        _router_body,
        grid=(t // tm,),
        in_specs=[pl.BlockSpec((tm, D_MODEL), lambda i: (i, 0)),
                  const((1, D_MODEL)), const((D_MODEL, LANES)), const((1, LANES)),
                  const((SUBLANES, LANES))],
        out_specs=[o[0] for o in outs] + [const((SUBLANES, LANES))],
        out_shape=[o[1] for o in outs] + [jax.ShapeDtypeStruct((SUBLANES, LANES), F32)],
        scratch_shapes=[pltpu.VMEM((SUBLANES, LANES), F32)],
        compiler_params=_cparams("arbitrary"),
        name="moe_router",
    )(x, g, wr, br, cnt0)


def _experts_body(be_ref, nused_ref, x_ref, wgu_ref, bgu_ref, wd_ref, bd_ref, o_ref,
                  wgu_bf, wd_bf):
    i = pl.program_id(0)
    live = i < nused_ref[0]

    @pl.when(live & ((i == 0) | (be_ref[i] != be_ref[jnp.maximum(i - 1, 0)])))
    def _():
        wgu_bf[...] = wgu_ref[0].astype(BF16)
        wd_bf[...] = wd_ref[0].astype(BF16)

    @pl.when(live)
    def _():
        x = x_ref[...].reshape(MOE_BM, D_MODEL).astype(BF16)
        gu = jnp.dot(x, wgu_bf[...], preferred_element_type=F32) + bgu_ref[0]
        x_glu = jnp.minimum(gu[:, :D_EXPERT], SWIGLU_LIMIT)
        x_lin = jnp.clip(gu[:, D_EXPERT:], -SWIGLU_LIMIT, SWIGLU_LIMIT)
        act = (x_lin + 1.0) * (x_glu * jax.nn.sigmoid(SWIGLU_ALPHA * x_glu))
        y = jnp.dot(act.astype(BF16), wd_bf[...], preferred_element_type=F32) + bd_ref[0]
        o_ref[...] = y.reshape(o_ref.shape)

    @pl.when(jnp.logical_not(live))
    def _():
        o_ref[...] = jnp.zeros(o_ref.shape, o_ref.dtype)


def _experts(x_sorted, block_expert, n_used, w_gu, b_gu, w_down, b_down, *, layer):
    n_slots = x_sorted.shape[0]
    n_blocks = n_slots // MOE_BM
    grid_spec = pltpu.PrefetchScalarGridSpec(
        num_scalar_prefetch=2,
        grid=(n_blocks,),
        in_specs=[
            pl.BlockSpec((MOE_BM, SUBLANES, LANES), lambda i, be, nu: (i, 0, 0)),
            pl.BlockSpec((None, 1, D_MODEL, 2 * D_EXPERT), lambda i, be, nu: (layer, be[i], 0, 0)),
            pl.BlockSpec((None, 1, 1, 2 * D_EXPERT), lambda i, be, nu: (layer, be[i], 0, 0)),
            pl.BlockSpec((None, 1, D_EXPERT, D_MODEL), lambda i, be, nu: (layer, be[i], 0, 0)),
            pl.BlockSpec((None, 1, 1, D_MODEL), lambda i, be, nu: (layer, be[i], 0, 0)),
        ],
        out_specs=pl.BlockSpec((MOE_BM, SUBLANES, LANES), lambda i, be, nu: (i, 0, 0)),
        scratch_shapes=[pltpu.VMEM((D_MODEL, 2 * D_EXPERT), BF16),
                        pltpu.VMEM((D_EXPERT, D_MODEL), BF16)],
    )
    return pl.pallas_call(
        _experts_body,
        grid_spec=grid_spec,
        out_shape=jax.ShapeDtypeStruct((n_slots, SUBLANES, LANES), F32),
        compiler_params=_cparams("arbitrary"),
        name="moe_experts",
    )(block_expert, n_used, x_sorted, w_gu, b_gu, w_down, b_down)


def _dispatch_body(slots_ref, x_ref, init_hbm, o_hbm, sem):
    del init_hbm
    tb = x_ref.shape[0]

    def issue(r, c):
        for k in range(TOP_K):
            pltpu.make_async_copy(x_ref.at[r], o_hbm.at[slots_ref[r * TOP_K + k]],
                                  sem).start(priority=k % 2)
        return c

    lax.fori_loop(0, tb, issue, 0)
    for k in range(TOP_K):
        pltpu.make_async_copy(x_ref, o_hbm.at[pl.ds(0, tb)], sem).wait()


def _dispatch(xn, slots_flat, x_sorted, *, tb):
    t = xn.shape[0]
    return pl.pallas_call(
        _dispatch_body,
        grid=(t // tb,),
        in_specs=[
            pl.BlockSpec((tb * TOP_K,), lambda i: (i,), memory_space=pltpu.SMEM),
            pl.BlockSpec((tb, SUBLANES, LANES), lambda i: (i, 0, 0)),
            pl.BlockSpec(memory_space=pl.ANY),
        ],
        out_specs=pl.BlockSpec(memory_space=pl.ANY),
        out_shape=jax.ShapeDtypeStruct(x_sorted.shape, x_sorted.dtype),
        scratch_shapes=[pltpu.SemaphoreType.DMA(())],
        input_output_aliases={2: 0},
        compiler_params=_cparams("arbitrary"),
        name="moe_dispatch",
    )(slots_flat, xn, x_sorted)


def _combine_body(slots_ref, h_ref, gate_ref, y_hbm, o_ref, buf, sem):
    tb = h_ref.shape[0]

    def row_copy(src_row, k, r):
        return pltpu.make_async_copy(y_hbm.at[src_row], buf.at[k * tb + r], sem)

    def issue(r, c):
        for k in range(TOP_K):
            row_copy(slots_ref[r * TOP_K + k], k, r).start(priority=k % 2)
        return c

    lax.fori_loop(0, tb, issue, 0)
    for k in range(TOP_K):
        pltpu.make_async_copy(y_hbm.at[pl.ds(0, tb)], buf.at[pl.ds(k * tb, tb)], sem).wait()
    rows = buf[...].reshape(TOP_K * tb, D_MODEL)
    g = gate_ref[...]
    acc = h_ref[...]
    for k in range(TOP_K):
        acc = acc + g[:, k:k + 1] * rows[k * tb:(k + 1) * tb, :]
    o_ref[...] = acc


def _combine(h, gate, slots_flat, y_slots, *, tb):
    t = h.shape[0]
    return pl.pallas_call(
        _combine_body,
        grid=(t // tb,),
        in_specs=[
            pl.BlockSpec((tb * TOP_K,), lambda i: (i,), memory_space=pltpu.SMEM),
            pl.BlockSpec((tb, D_MODEL), lambda i: (i, 0)),
            pl.BlockSpec((tb, LANES), lambda i: (i, 0)),
            pl.BlockSpec(memory_space=pl.ANY),
        ],
        out_specs=pl.BlockSpec((tb, D_MODEL), lambda i: (i, 0)),
        out_shape=jax.ShapeDtypeStruct((t, D_MODEL), F32),
        scratch_shapes=[pltpu.VMEM((TOP_K * tb, SUBLANES, LANES), F32),
                        pltpu.SemaphoreType.DMA(())],
        compiler_params=_cparams("arbitrary"),
        name="moe_combine",
    )(slots_flat, h, gate, y_slots)


def _moe(h_streams, g, wr, br, w_gu, b_gu, w_down, b_down, *, layer):
    cnt = jnp.zeros((SUBLANES, LANES), F32)
    routed = []
    for h in h_streams:
        xn, idx, gate, pos, cnt = _router(h, g, wr, br, cnt, tm=256)
        routed.append((xn, idx[:, :TOP_K], gate, pos[:, :TOP_K]))
    counts = cnt[0, :N_EXPERTS].astype(jnp.int32)
    padded = (counts + MOE_BM - 1) // MOE_BM * MOE_BM
    pad_end = jnp.cumsum(padded)
    pad_start = pad_end - padded
    t_all = sum(h.shape[0] for h in h_streams)
    n_blocks = -(-(t_all * TOP_K + N_EXPERTS * (MOE_BM - 1)) // MOE_BM)
    n_slots = n_blocks * MOE_BM
    n_used = (pad_end[-1] // MOE_BM).astype(jnp.int32)
    blk = jnp.minimum(jnp.arange(n_blocks, dtype=jnp.int32), n_used - 1) * MOE_BM
    block_expert = jnp.minimum(jnp.sum(pad_end[None, :] <= blk[:, None], axis=1),
                               N_EXPERTS - 1).astype(jnp.int32)
    slots = [(pad_start[idx] + pos).reshape(-1) for (_, idx, _, pos) in routed]
    x_sorted = jnp.zeros((n_slots, SUBLANES, LANES), F32)
    for (xn, _, _, _), s in zip(routed, slots):
        x_sorted = _dispatch(xn, s, x_sorted, tb=256)
    y_slots = _experts(x_sorted, block_expert, n_used.reshape(1), w_gu, b_gu, w_down, b_down,
                       layer=layer)
    return [_combine(h, gate, s, y_slots, tb=256)
            for h, s, (_, _, gate, _) in zip(h_streams, slots, routed)]


LOG2E = 1.0 / math.log(2.0)


def _softplus2(y):
    neg_abs = lax.bitcast_convert_type(
        lax.bitcast_convert_type(y, jnp.uint32) | jnp.uint32(0x80000000), F32)
    return jnp.maximum(y, 0.0) + jnp.log(1.0 + jnp.exp2(neg_abs)) * LOG2E


def _sb_block(y, v, r_in, suffix_mat, mask, v_transposed=False):
    sp = _softplus2(y)
    if mask is not None:
        sp = jnp.where(mask, sp, 0.0)
    s_incl = jnp.dot(sp.astype(BF16), suffix_mat, preferred_element_type=F32) + r_in
    w = jnp.exp2(y - s_incl)
    if mask is not None:
        w = jnp.where(mask, w, 0.0)
    dims = (((1,), (1 if v_transposed else 0,)), ((), ()))
    return (lax.dot_general(w.astype(BF16), v, dims, preferred_element_type=F32),
            s_incl[:, 0:1])


def _suffix_matrix(n):
    return (lax.broadcasted_iota(jnp.int32, (n, n), 0)
            >= lax.broadcasted_iota(jnp.int32, (n, n), 1)).astype(BF16)


def _attn_prompt_body(bias_ref, q_ref, k_ref, v_ref, o_ref, y_scr, sp_scr):
    hp = pl.program_id(1)
    qi = pl.program_id(2)
    tq, tk = ATTN_TQ, ATTN_TK
    q = q_ref[...]
    lane = lax.broadcasted_iota(jnp.int32, (tq, LANES), 1)
    first = lane < HEAD_DIM
    suffix = _suffix_matrix(tk)
    diag_mask = (lax.broadcasted_iota(jnp.int32, (tq, tk), 1)
                 < lax.broadcasted_iota(jnp.int32, (tq, tk), 0))
    heads = ((jnp.where(first, q, jnp.zeros_like(q)), bias_ref[2 * hp]),
             (jnp.where(first, jnp.zeros_like(q), q), bias_ref[2 * hp + 1]))

    def logits(j, masked):
        kb = k_ref[pl.ds(pl.multiple_of(j * tk, tk), tk), :]
        ys = []
        for qh, bias in heads:
            y = lax.dot_general(qh, kb, (((1,), (1,)), ((), ())),
                                preferred_element_type=F32) + bias
            if masked:
                y = jnp.where(diag_mask, y, NEG_BIG)
            ys.append(y)
        return ys

    def split_store(ys, slot):
        for h, y in enumerate(ys):
            y_scr[h, slot] = y
            sp_scr[h, slot] = _softplus2(y).astype(BF16)

    def suffix_sums(slot):
        return [jnp.dot(sp_scr[h, slot], suffix, preferred_element_type=F32) for h in range(2)]

    def finish(s2s, j, slot, carry):
        vb = v_ref[pl.ds(pl.multiple_of(j * tk, tk), tk), :]
        outs, rs = [], []
        for h, s2 in enumerate(s2s):
            s_incl = s2 + carry[h]
            w = jnp.exp2(y_scr[h, slot] - s_incl)
            outs.append(jnp.dot(w.astype(BF16), vb, preferred_element_type=F32))
            rs.append(s_incl[:, 0:1])
        return rs[0], rs[1], carry[2] + jnp.where(first, outs[0], outs[1])

    def step(j_next, j_cur, cur_slot, carry):
        ys = logits(j_next, False)
        s2s = suffix_sums(cur_slot)
        split_store(ys, 1 - cur_slot)
        return finish(s2s, j_cur, cur_slot, carry)

    def steps(j, n, carry):
        for i in range(n):
            carry = step(j - i - 1, j - i, i % 2, carry)
        return carry

    unroll = ATTN_UNROLL
    split_store(logits(qi, True), 0)
    carry = (jnp.zeros((tq, 1), F32), jnp.zeros((tq, 1), F32), jnp.zeros((tq, LANES), F32))
    carry = lax.fori_loop(0, qi // unroll, lambda u, c: steps(qi - unroll * u, unroll, c), carry)
    rest = qi % unroll
    carry = lax.fori_loop(0, rest // 2, lambda u, c: steps(rest - 2 * u, 2, c), carry)
    odd = qi % 2

    def tail(c):
        c = step(0, 1, 0, c)
        return finish(suffix_sums(1), 0, 1, c)

    def last(c):
        return finish(suffix_sums(0), 0, 0, c)

    carry = lax.cond(odd == 1, tail, last, carry)
    o_ref[...] = carry[2].astype(o_ref.dtype)


def _attn_prompt(q, k, v, bias, *, n_seq):
    t = q.shape[0]
    l = t // n_seq
    assert ATTN_TQ == ATTN_TK and l % ATTN_TQ == 0
    nq = l // ATTN_TQ
    grid_spec = pltpu.PrefetchScalarGridSpec(
        num_scalar_prefetch=0,
        grid=(n_seq, N_HEADS // 2, nq),
        in_specs=[
            pl.BlockSpec(memory_space=pltpu.SMEM),
            pl.BlockSpec((ATTN_TQ, LANES), lambda b, hp, qi: (b * nq + qi, hp)),
            pl.BlockSpec((l, LANES), lambda b, hp, qi: (b, hp)),
            pl.BlockSpec((l, LANES), lambda b, hp, qi: (b, hp)),
        ],
        out_specs=pl.BlockSpec((ATTN_TQ, LANES), lambda b, hp, qi: (b * nq + qi, hp)),
        scratch_shapes=[pltpu.VMEM((2, 2, ATTN_TQ, ATTN_TK), F32),
                        pltpu.VMEM((2, 2, ATTN_TQ, ATTN_TK), BF16)],
    )
    return pl.pallas_call(
        _attn_prompt_body,
        grid_spec=grid_spec,
        out_shape=jax.ShapeDtypeStruct((t, N_HEADS * HEAD_DIM), BF16),
        compiler_params=_cparams("parallel", "parallel", "arbitrary"),
        name="sb_attn_prompt",
    )(bias, q, k, v)


def _attn_sample_body(pt_ref, qbd_ref, kn_ref, vn_ref, bias_ref, *refs):
    pages = PAGES_PER_STEP
    k_refs = refs[:pages]
    v_refs = refs[pages:2 * pages]
    o_ref = refs[2 * pages]
    acc_ref, r_ref = refs[2 * pages + 1:]
    s = pl.program_id(1)
    n_rows = qbd_ref.shape[1]
    n_new = n_rows // N_HEADS
    qbd = qbd_ref[0]
    bias = bias_ref[...]
    suffix = _suffix_matrix(PAGE_SIZE)

    def logits(kb):
        return lax.dot_general(qbd, kb, (((1,), (1,)), ((), ())), preferred_element_type=F32) + bias

    @pl.when(s == 0)
    def _():
        col = lax.broadcasted_iota(jnp.int32, (n_rows, PAGE_SIZE), 1)
        t_of_row = lax.broadcasted_iota(jnp.int32, (n_rows, PAGE_SIZE), 0) % n_new
        contrib, r = _sb_block(logits(kn_ref[0]), vn_ref[0], jnp.zeros((n_rows, 1), F32),
                               suffix, col < t_of_row)
        acc_ref[...] = contrib
        r_ref[...] = jnp.broadcast_to(r, r_ref.shape)

    def page(ref):
        return ref[0].reshape(N_HEADS * HEAD_DIM, PAGE_SIZE).astype(BF16)

    for i in range(pages):
        y = jnp.dot(qbd, page(k_refs[i]), preferred_element_type=F32) + bias
        contrib, r = _sb_block(y, page(v_refs[i]), r_ref[:, 0:1], suffix, None,
                               v_transposed=True)
        acc_ref[...] += contrib
        r_ref[...] = jnp.broadcast_to(r, r_ref.shape)

    @pl.when(s == pl.num_programs(1) - 1)
    def _():
        lane_head = lax.broadcasted_iota(jnp.int32, (n_new, N_HEADS * HEAD_DIM), 1) // HEAD_DIM
        out = jnp.zeros((n_new, N_HEADS * HEAD_DIM), F32)
        for h in range(N_HEADS):
            out = jnp.where(lane_head == h, acc_ref[h * n_new:(h + 1) * n_new, :], out)
        o_ref[0] = out.astype(o_ref.dtype)


def _attn_sample(q, k_new, v_new, bias, cache_k, cache_v, page_table):
    b, n_new, d = q.shape
    n_pages = page_table.shape[1]
    pages = PAGES_PER_STEP
    assert n_pages % pages == 0 and n_new <= PAGE_SIZE
    n_rows = N_HEADS * n_new
    q4 = q.reshape(b, n_new, N_HEADS, HEAD_DIM).transpose(0, 2, 1, 3)
    eye = jnp.eye(N_HEADS, dtype=q.dtype)
    qbd = (q4[:, :, :, None, :] * eye[None, :, None, :, None]).reshape(b, n_rows, d)
    pad = lambda x: jnp.pad(x, ((0, 0), (0, PAGE_SIZE - n_new), (0, 0)))
    bias_rows = jnp.broadcast_to(jnp.repeat(bias, n_new)[:, None], (n_rows, PAGE_SIZE)).astype(F32)

    def page_spec(i):
        return pl.BlockSpec((1, N_HEADS, HEAD_DIM, PAGE_SIZE),
                            lambda bb, s, pt: (pt[bb, n_pages - 1 - (s * pages + i)], 0, 0, 0))

    grid_spec = pltpu.PrefetchScalarGridSpec(
        num_scalar_prefetch=1,
        grid=(b, n_pages // pages),
        in_specs=[
            pl.BlockSpec((1, n_rows, d), lambda bb, s, pt: (bb, 0, 0)),
            pl.BlockSpec((1, PAGE_SIZE, d), lambda bb, s, pt: (bb, 0, 0)),
            pl.BlockSpec((1, PAGE_SIZE, d), lambda bb, s, pt: (bb, 0, 0)),
            pl.BlockSpec((n_rows, PAGE_SIZE), lambda bb, s, pt: (0, 0)),
        ] + [page_spec(i) for i in range(pages)] + [page_spec(i) for i in range(pages)],
        out_specs=pl.BlockSpec((1, n_new, d), lambda bb, s, pt: (bb, 0, 0)),
        scratch_shapes=[pltpu.VMEM((n_rows, d), F32), pltpu.VMEM((n_rows, LANES), F32)],
    )
    return pl.pallas_call(
        _attn_sample_body,
        grid_spec=grid_spec,
        out_shape=jax.ShapeDtypeStruct((b, n_new, d), BF16),
        compiler_params=_cparams("parallel", "arbitrary"),
        name="sb_attn_sample",
    )(page_table, qbd, pad(k_new), pad(v_new), bias_rows,
      *([cache_k] * pages), *([cache_v] * pages))


def _rmsnorm_body(x_ref, g_ref, o_ref):
    x = x_ref[...]
    ms = jnp.mean(x * x, axis=-1, keepdims=True)
    o_ref[...] = x * lax.rsqrt(ms + RMS_EPS) * g_ref[...]


def _rmsnorm(x, g, *, tm):
    t, d = x.shape
    return pl.pallas_call(
        _rmsnorm_body,
        grid=(t // tm,),
        in_specs=[pl.BlockSpec((tm, d), lambda i: (i, 0)), pl.BlockSpec((1, d), lambda i: (0, 0))],
        out_specs=pl.BlockSpec((tm, d), lambda i: (i, 0)),
        out_shape=jax.ShapeDtypeStruct((t, d), F32),
        compiler_params=_cparams("parallel"),
        name="final_rmsnorm",
    )(x, g)


def _pad_lanes(x, n=LANES, value=0.0):
    return jnp.pad(x, [(0, 0)] * (x.ndim - 1) + [(0, n - x.shape[-1])], constant_values=value)


def kernel(x_prompt, x_sample, state_conv, state_ssm, cache_k, cache_v, page_table, g_mix, w_in, conv_w, conv_b, dt_bias, a_log, d_skip, g_ssm_norm, w_ssm_out, g_kv, w_k, w_v, g_attn, w_q, w_o, sb_bias, g_ffn, w_router, b_router, w_gu, b_gu, w_down, b_down, g_final):
    bp, lp, d = x_prompt.shape
    bs, ls, _ = x_sample.shape
    tp, ts = bp * lp, bs * ls
    streams = [x_prompt.reshape(tp, d), x_sample.reshape(ts, d)]
    tms = [1024, ts]

    w_in0 = w_in[0] * g_mix[0][:, None]
    w_zx = w_in0[:, :D_INNER + CONV_DIM].astype(BF16)
    w_dt = _pad_lanes(w_in0[:, D_INNER + CONV_DIM:])
    w_out = w_ssm_out[0].astype(BF16)
    w_q1 = (w_q[0] * g_attn[0][:, None] * (HEAD_DIM ** -0.5 * LOG2E)).astype(BF16)
    bias2 = sb_bias[0] * LOG2E
    w_k1 = (w_k * g_kv[:, None]).astype(BF16)
    w_v1 = (w_v * g_kv[:, None]).astype(BF16)
    w_o1 = w_o[0].astype(BF16)
    b_gu4 = b_gu[:, :, None, :]
    b_down4 = b_down[:, :, None, :]
    wr = _pad_lanes(w_router)
    br = _pad_lanes(b_router, value=NEG_BIG)[:, None, :]
    cw = jnp.pad(conv_w[0], ((0, SUBLANES - CONV_W), (0, 0)))
    expand_mat = (jnp.arange(D_INNER)[None, :] // SSM_HEAD_DIM
                  == jnp.arange(LANES)[:, None]).astype(BF16)
    dskip_x = jnp.repeat(d_skip[0], SSM_HEAD_DIM)[None, :]
    conv_pad = lambda cs: jnp.pad(cs, ((0, 0), (SUBLANES - (CONV_W - 1), 0), (0, 0)))
    conv0 = [jnp.zeros((bp, SUBLANES, CONV_DIM), F32), conv_pad(state_conv[0])]
    ssm0 = [jnp.zeros((bp, D_INNER, D_STATE), F32), state_ssm[0].reshape(bs, D_INNER, D_STATE)]

    new_conv, new_ssm, h1 = [], [], []
    for i, (h, tm) in enumerate(zip(streams, tms)):
        n_seq, l = (bp, lp) if i == 0 else (bs, ls)
        zx, = _norm_matmul(h, w_zx, norm=True, tm=tm, tn=1024, name="in_proj")
        dt_raw, = _norm_matmul(h, w_dt, norm=True, tm=tm, tn=LANES, precision=HIGHEST,
                               name="in_proj_dt")
        if l % SSD_CHUNK:
            assert l < SSD_CHUNK and l % SUBLANES == 0
            padrows = lambda x: jnp.pad(x.reshape(n_seq, l, -1), ((0, 0), (0, SSD_CHUNK - l), (0, 0))
                                        ).reshape(n_seq * SSD_CHUNK, -1)
            zx, dt_raw = padrows(zx), padrows(dt_raw)
            valid = l
        else:
            valid = SSD_CHUNK
        y, conv_o, ssm_o = _ssd(zx, dt_raw, conv0[i], ssm0[i], cw, conv_b[0][None, :],
                                _pad_lanes(dt_bias[0])[None, :], _pad_lanes(a_log[0])[None, :],
                                dskip_x, g_ssm_norm[0][None, :], expand_mat,
                                n_seq=n_seq, valid_len=valid)
        if l % SSD_CHUNK:
            y = y.reshape(n_seq, SSD_CHUNK, D_INNER)[:, :l].reshape(n_seq * l, D_INNER)
        new_conv.append(conv_o[:, SUBLANES - (CONV_W - 1):][None])
        new_ssm.append(ssm_o.reshape(n_seq, N_SSM_HEADS, SSM_HEAD_DIM, D_STATE)[None])
        hm, = _norm_matmul(y, w_out, norm=False, res=h, tm=tm, tn=1024, name="ssm_out_proj")
        h1.append(hm)
    h1 = _moe(h1, g_ffn[0][None, :], wr[0], br[0], w_gu, b_gu4, w_down, b_down4, layer=0)

    qs, ks, vs, kbs, vbs = [], [], [], [], []
    for h, tm in zip(h1, tms):
        q, = _norm_matmul(h, w_q1, norm=True, out_dtypes=(BF16,), tm=tm, tn=1024, name="q_proj")
        k, kb = _norm_matmul(h, w_k1, norm=True, out_dtypes=(F32, BF16), tm=min(tm, 512), tn=1024,
                             split_heads=True, name="k_proj")
        v, vb = _norm_matmul(h, w_v1, norm=True, out_dtypes=(F32, BF16), tm=min(tm, 512), tn=1024,
                             split_heads=True, name="v_proj")
        qs.append(q); ks.append(k); vs.append(v); kbs.append(kb); vbs.append(vb)
    o_p = _attn_prompt(qs[0], kbs[0], vbs[0], bias2, n_seq=bp)
    n_phys = cache_k.shape[0]
    o_s = _attn_sample(qs[1].reshape(bs, ls, d), kbs[1].reshape(bs, ls, d), vbs[1].reshape(bs, ls, d),
                       bias2, cache_k.transpose(0, 2, 3, 1), cache_v.transpose(0, 2, 3, 1),
                       page_table).reshape(ts, d)
    h2 = []
    for h, o, tm in zip(h1, [o_p, o_s], tms):
        hm, = _norm_matmul(o, w_o1, norm=False, res=h, tm=tm, tn=1024, name="attn_out_proj")
        h2.append(hm)
    h2 = _moe(h2, g_ffn[1][None, :], wr[1], br[1], w_gu, b_gu4, w_down, b_down4, layer=1)

    y_p = _rmsnorm(h2[0], g_final[None, :], tm=1024).reshape(bp, lp, d)
    y_s = _rmsnorm(h2[1], g_final[None, :], tm=ts).reshape(bs, ls, d)
    kv_shape = lambda x, b, l: x.reshape(b, l, N_HEADS, HEAD_DIM)
    return (y_p, y_s, new_conv[0], new_ssm[0], kv_shape(ks[0], bp, lp), kv_shape(vs[0], bp, lp),
            new_conv[1], new_ssm[1], kv_shape(ks[1], bs, ls), kv_shape(vs[1], bs, ls))
```

```python
import functools
import math

import jax
import jax.numpy as jnp
from jax import lax
from jax.experimental import pallas as pl
from jax.experimental.pallas import tpu as pltpu

F32 = jnp.float32
BF16 = jnp.bfloat16
HIGHEST = lax.Precision.HIGHEST

LANES = 128
SUBLANES = 8
VMEM_LIMIT = 48 * 1024 * 1024

D_MODEL = 1024
D_INNER = 2048
SSM_HEAD_DIM = 64
N_SSM_HEADS = 32
D_STATE = 128
N_GROUPS = 8
GROUP_W = D_INNER // N_GROUPS
CONV_W = 4
CONV_DIM = D_INNER + 2 * N_GROUPS * D_STATE
SSD_CHUNK = 128
HEAD_DIM = 64
N_HEADS = 16
N_EXPERTS = 32
TOP_K = 4
D_EXPERT = 1024
SWIGLU_ALPHA = 1.702
SWIGLU_LIMIT = 7.0
RMS_EPS = 1e-5
PAGE_SIZE = 128

MOE_BM = 256
ATTN_TQ = 256
ATTN_TK = 256
ATTN_UNROLL = 8
PAGES_PER_STEP = 4
NEG_BIG = -1e30


def _cparams(*sem, flags=None):
    return pltpu.CompilerParams(dimension_semantics=sem, vmem_limit_bytes=VMEM_LIMIT, flags=flags)


def _softplus(z):
    return jnp.maximum(z, 0.0) + jnp.log1p(jnp.exp(-jnp.abs(z)))


def _split_bf16(x):
    hi = x.astype(BF16)
    lo = (x - hi.astype(F32)).astype(BF16)
    return hi, lo


def _nm_body(*refs, norm, has_res, n_out, precision):
    x_ref, w_ref = refs[0], refs[1]
    pos = 2
    res_ref = None
    if has_res:
        res_ref = refs[pos]
        pos += 1
    out_refs = refs[pos:pos + n_out]
    pos += n_out
    if norm:
        xn_ref = refs[pos]

        @pl.when(pl.program_id(1) == 0)
        def _():
            x = x_ref[...].astype(F32)
            ms = jnp.mean(x * x, axis=-1, keepdims=True)
            xn_ref[...] = (x * lax.rsqrt(ms + RMS_EPS)).astype(xn_ref.dtype)

        lhs = xn_ref[...]
    else:
        lhs = x_ref[...]
    acc = jnp.dot(lhs, w_ref[...], preferred_element_type=F32, precision=precision)
    if has_res:
        acc = acc + res_ref[...]
    for o in out_refs:
        o[...] = acc.astype(o.dtype).reshape(o.shape)


def _norm_matmul(x, w, *, norm, res=None, out_dtypes=(F32,), tm, tn, precision=None,
                 split_heads=False, name="matmul"):
    t, k = x.shape
    n = w.shape[1]
    assert t % tm == 0 and n % tn == 0
    in_specs = [pl.BlockSpec((tm, k), lambda i, j: (i, 0)),
                pl.BlockSpec((k, tn), lambda i, j: (0, j))]
    args = [x, w]
    if res is not None:
        in_specs.append(pl.BlockSpec((tm, tn), lambda i, j: (i, j)))
        args.append(res)
    out_specs = [pl.BlockSpec((tm, tn), lambda i, j: (i, j)) for _ in out_dtypes]
    out_shape = [jax.ShapeDtypeStruct((t, n), d) for d in out_dtypes]
    if split_heads:
        assert n == tn == N_HEADS * HEAD_DIM
        out_specs[0] = pl.BlockSpec((tm, N_HEADS, HEAD_DIM), lambda i, j: (i, 0, 0))
        out_shape[0] = jax.ShapeDtypeStruct((t, N_HEADS, HEAD_DIM), out_dtypes[0])
    outs = pl.pallas_call(
        functools.partial(_nm_body, norm=norm, has_res=res is not None,
                          n_out=len(out_dtypes), precision=precision),
        grid=(t // tm, n // tn),
        in_specs=in_specs,
        out_specs=out_specs,
        out_shape=out_shape,
        scratch_shapes=[pltpu.VMEM((tm, k), w.dtype)] if norm else [],
        compiler_params=_cparams("parallel", "arbitrary"),
        name=name,
    )(*args)
    return outs


def _ssd_body(zx_ref, dt_ref, conv0_ref, ssm0_ref, cw_ref, cb_ref, dtb_ref, alog_ref,
              dskip_ref, gn_ref, e_ref,
              y_ref, conv_out_ref, ssm_ref,
              ext_ref, xbc_ref, xg_ref, xd_ref, *, valid_len):
    L = SSD_CHUNK
    c = pl.program_id(1)

    @pl.when(c == 0)
    def _():
        ext_ref[0:SUBLANES, :] = conv0_ref[0]
        ssm_ref[0] = ssm0_ref[0]

    ext_ref[SUBLANES:SUBLANES + L, :] = zx_ref[:, D_INNER:D_INNER + CONV_DIM]
    strip = 512
    for s in range(CONV_DIM // strip):
        cols = slice(s * strip, (s + 1) * strip)
        acc = jnp.broadcast_to(cb_ref[:, cols], (L, strip))
        for w in range(CONV_W):
            r0 = SUBLANES - (CONV_W - 1) + w
            acc = acc + ext_ref[r0:r0 + L, cols] * cw_ref[w:w + 1, cols]
        xbc_ref[:, cols] = acc * jax.nn.sigmoid(acc)
    conv_out_ref[0] = ext_ref[valid_len:valid_len + SUBLANES, :]
    ext_ref[0:SUBLANES, :] = ext_ref[L:L + SUBLANES, :]

    lane = lax.broadcasted_iota(jnp.int32, (L, LANES), 1)
    row = lax.broadcasted_iota(jnp.int32, (L, LANES), 0)
    live = (lane < N_SSM_HEADS) & (row < valid_len)
    dt = jnp.where(live, _softplus(dt_ref[...] + dtb_ref[...]), 0.0)
    a = dt * (-jnp.exp(alog_ref[...]))
    tril = (lax.broadcasted_iota(jnp.int32, (L, L), 0)
            >= lax.broadcasted_iota(jnp.int32, (L, L), 1))
    a_cum = jnp.dot(tril.astype(F32), a, preferred_element_type=F32, precision=HIGHEST)
    a_cum_t = a_cum.T
    a_last = a_cum[L - 1:L, :]
    ea = jnp.exp(a_cum)
    w_end = dt * jnp.exp(a_last - a_cum)
    chunk_decay_t = jnp.exp(a_cum_t[:, L - 1:L])

    def expand(p):
        hi, lo = _split_bf16(p)
        e = e_ref[...]
        return (jnp.dot(hi, e, preferred_element_type=F32)
                + jnp.dot(lo, e, preferred_element_type=F32))

    dt_x = expand(dt)
    wend_x = expand(w_end)
    ea_x = expand(ea)
    xs = xbc_ref[:, 0:D_INNER]
    xg_ref[...] = (xs * dt_x).astype(BF16)
    xd_ref[...] = xs * wend_x

    for g in range(N_GROUPS):
        gcols = slice(g * GROUP_W, (g + 1) * GROUP_W)
        b_g = xbc_ref[:, D_INNER + g * D_STATE:D_INNER + (g + 1) * D_STATE].astype(BF16)
        c_g = xbc_ref[:, D_INNER + N_GROUPS * D_STATE + g * D_STATE:
                      D_INNER + N_GROUPS * D_STATE + (g + 1) * D_STATE].astype(BF16)
        cb = lax.dot_general(c_g, b_g, (((1,), (1,)), ((), ())), preferred_element_type=F32)
        s_g = ssm_ref[0, g * GROUP_W:(g + 1) * GROUP_W, :]
        y_off = lax.dot_general(c_g, s_g.astype(BF16), (((1,), (1,)), ((), ())),
                                preferred_element_type=F32) * ea_x[:, gcols]
        y_parts = []
        for hh in range(GROUP_W // SSM_HEAD_DIM):
            h = g * (GROUP_W // SSM_HEAD_DIM) + hh
            seg = a_cum[:, h:h + 1] - a_cum_t[h:h + 1, :]
            m = (cb * jnp.where(tril, jnp.exp(seg), 0.0)).astype(BF16)
            y_parts.append(jnp.dot(m, xg_ref[:, h * SSM_HEAD_DIM:(h + 1) * SSM_HEAD_DIM],
                                   preferred_element_type=F32))
        y = jnp.concatenate(y_parts, axis=1) + y_off
        y = y + xbc_ref[:, gcols] * dskip_ref[:, gcols]
        xd_t = xd_ref[:, gcols].T.astype(BF16)
        contrib = jnp.dot(xd_t, b_g, preferred_element_type=F32)
        for hh in range(GROUP_W // SSM_HEAD_DIM):
            h = g * (GROUP_W // SSM_HEAD_DIM) + hh
            rows = slice(hh * SSM_HEAD_DIM, (hh + 1) * SSM_HEAD_DIM)
            ssm_ref[0, g * GROUP_W + hh * SSM_HEAD_DIM:g * GROUP_W + (hh + 1) * SSM_HEAD_DIM, :] = (
                s_g[rows, :] * chunk_decay_t[h:h + 1, :] + contrib[rows, :])
        z = zx_ref[:, gcols]
        yz = y * (z * jax.nn.sigmoid(z))
        ms = jnp.mean(yz * yz, axis=-1, keepdims=True)
        y_ref[:, gcols] = (yz * lax.rsqrt(ms + RMS_EPS) * gn_ref[:, gcols]).astype(y_ref.dtype)


def _ssd(zx, dt_raw, conv0, ssm0, cw, cb, dtb, alog, dskip, gn, expand_mat, *, n_seq, valid_len):
    rows = zx.shape[0]
    cps = rows // n_seq // SSD_CHUNK
    width = zx.shape[1]
    const = lambda shape: pl.BlockSpec(shape, lambda b, c: (0,) * len(shape))
    return pl.pallas_call(
        functools.partial(_ssd_body, valid_len=valid_len),
        grid=(n_seq, cps),
        in_specs=[
            pl.BlockSpec((SSD_CHUNK, width), lambda b, c: (b * cps + c, 0)),
            pl.BlockSpec((SSD_CHUNK, LANES), lambda b, c: (b * cps + c, 0)),
            pl.BlockSpec((1, SUBLANES, CONV_DIM), lambda b, c: (b, 0, 0)),
            pl.BlockSpec((1, D_INNER, D_STATE), lambda b, c: (b, 0, 0)),
            const((SUBLANES, CONV_DIM)), const((1, CONV_DIM)), const((1, LANES)), const((1, LANES)),
            const((1, D_INNER)), const((1, D_INNER)), const((LANES, D_INNER)),
        ],
        out_specs=[
            pl.BlockSpec((SSD_CHUNK, D_INNER), lambda b, c: (b * cps + c, 0)),
            pl.BlockSpec((1, SUBLANES, CONV_DIM), lambda b, c: (b, 0, 0)),
            pl.BlockSpec((1, D_INNER, D_STATE), lambda b, c: (b, 0, 0)),
        ],
        out_shape=[
            jax.ShapeDtypeStruct((rows, D_INNER), BF16),
            jax.ShapeDtypeStruct((n_seq, SUBLANES, CONV_DIM), F32),
            jax.ShapeDtypeStruct((n_seq, D_INNER, D_STATE), F32),
        ],
        scratch_shapes=[
            pltpu.VMEM((SSD_CHUNK + SUBLANES, CONV_DIM), F32),
            pltpu.VMEM((SSD_CHUNK, CONV_DIM), F32),
            pltpu.VMEM((SSD_CHUNK, D_INNER), BF16),
            pltpu.VMEM((SSD_CHUNK, D_INNER), F32),
        ],
        compiler_params=_cparams("parallel", "arbitrary"),
        name="ssd_mixer",
    )(zx, dt_raw, conv0, ssm0, cw, cb, dtb, alog, dskip, gn, expand_mat)


def _router_body(x_ref, g_ref, wr_ref, br_ref, cnt0_ref,
                 xn_ref, idx_ref, gate_ref, pos_ref, cnt_ref, cnt_scr):
    tm = x_ref.shape[0]

    @pl.when(pl.program_id(0) == 0)
    def _():
        cnt_scr[...] = cnt0_ref[...]

    x = x_ref[...]
    ms = jnp.mean(x * x, axis=-1, keepdims=True)
    xh = x * lax.rsqrt(ms + RMS_EPS) * g_ref[...]
    xn_ref[...] = xh.reshape(xn_ref.shape)
    logits = jnp.dot(xh, wr_ref[...], preferred_element_type=F32, precision=HIGHEST) + br_ref[...]
    lane = lax.broadcasted_iota(jnp.int32, (tm, LANES), 1)
    lane_f = lane.astype(F32)
    work = logits
    vals, idxs, hots = [], [], []
    for _ in range(TOP_K):
        m = jnp.max(work, axis=-1, keepdims=True)
        am = jnp.min(jnp.where(work == m, lane_f, float(LANES)), axis=-1, keepdims=True)
        hot = lane_f == am
        vals.append(m)
        idxs.append(am)
        hots.append(hot)
        work = jnp.where(hot, -jnp.inf, work)
    es = [jnp.exp(v - vals[0]) for v in vals]
    den = es[0] + es[1] + es[2] + es[3]
    sel = (hots[0] | hots[1] | hots[2] | hots[3])
    sel_f = sel.astype(F32)
    before = (lax.broadcasted_iota(jnp.int32, (tm, tm), 0)
              > lax.broadcasted_iota(jnp.int32, (tm, tm), 1)).astype(BF16)
    rank = jnp.dot(before, sel_f.astype(BF16), preferred_element_type=F32) + cnt_scr[0:1, :]
    cnt_new = cnt_scr[0:1, :] + jnp.sum(sel_f, axis=0, keepdims=True)
    cnt_scr[...] = jnp.broadcast_to(cnt_new, cnt_scr.shape)
    cnt_ref[...] = jnp.broadcast_to(cnt_new, cnt_ref.shape)
    idx_o = jnp.zeros((tm, LANES), jnp.int32)
    gate_o = jnp.zeros((tm, LANES), F32)
    pos_o = jnp.zeros((tm, LANES), jnp.int32)
    for k in range(TOP_K):
        p_k = jnp.sum(jnp.where(hots[k], rank, 0.0), axis=-1, keepdims=True).astype(jnp.int32)
        idx_o = jnp.where(lane == k, idxs[k].astype(jnp.int32), idx_o)
        gate_o = jnp.where(lane == k, es[k] / den, gate_o)
        pos_o = jnp.where(lane == k, p_k, pos_o)
    idx_ref[...] = idx_o
    gate_ref[...] = gate_o
    pos_ref[...] = pos_o


def _router(x, g, wr, br, cnt0, *, tm):
    t = x.shape[0]
    row = lambda w, d: (pl.BlockSpec((tm, w), lambda i: (i, 0)), jax.ShapeDtypeStruct((t, w), d))
    tiles = (pl.BlockSpec((tm, SUBLANES, LANES), lambda i: (i, 0, 0)),
             jax.ShapeDtypeStruct((t, SUBLANES, LANES), F32))
    outs = [tiles, row(LANES, jnp.int32), row(LANES, F32), row(LANES, jnp.int32)]
    const = lambda shape: pl.BlockSpec(shape, lambda i: (0,) * len(shape))
    return pl.pallas_call(
        _router_body,
        grid=(t // tm,),
        in_specs=[pl.BlockSpec((tm, D_MODEL), lambda i: (i, 0)),
                  const((1, D_MODEL)), const((D_MODEL, LANES)), const((1, LANES)),
                  const((SUBLANES, LANES))],
        out_specs=[o[0] for o in outs] + [const((SUBLANES, LANES))],
        out_shape=[o[1] for o in outs] + [jax.ShapeDtypeStruct((SUBLANES, LANES), F32)],
        scratch_shapes=[pltpu.VMEM((SUBLANES, LANES), F32)],
        compiler_params=_cparams("arbitrary"),
        name="moe_router",
    )(x, g, wr, br, cnt0)


def _experts_body(be_ref, nused_ref, x_ref, wgu_ref, bgu_ref, wd_ref, bd_ref, o_ref,
                  wgu_bf, wd_bf):
    i = pl.program_id(0)
    live = i < nused_ref[0]

    @pl.when(live & ((i == 0) | (be_ref[i] != be_ref[jnp.maximum(i - 1, 0)])))
    def _():
        wgu_bf[...] = wgu_ref[0].astype(BF16)
        wd_bf[...] = wd_ref[0].astype(BF16)

    @pl.when(live)
    def _():
        x = x_ref[...].reshape(MOE_BM, D_MODEL).astype(BF16)
        gu = jnp.dot(x, wgu_bf[...], preferred_element_type=F32) + bgu_ref[0]
        x_glu = jnp.minimum(gu[:, :D_EXPERT], SWIGLU_LIMIT)
        x_lin = jnp.clip(gu[:, D_EXPERT:], -SWIGLU_LIMIT, SWIGLU_LIMIT)
        act = (x_lin + 1.0) * (x_glu * jax.nn.sigmoid(SWIGLU_ALPHA * x_glu))
        y = jnp.dot(act.astype(BF16), wd_bf[...], preferred_element_type=F32) + bd_ref[0]
        o_ref[...] = y.reshape(o_ref.shape)

    @pl.when(jnp.logical_not(live))
    def _():
        o_ref[...] = jnp.zeros(o_ref.shape, o_ref.dtype)


def _experts(x_sorted, block_expert, n_used, w_gu, b_gu, w_down, b_down, *, layer):
    n_slots = x_sorted.shape[0]
    n_blocks = n_slots // MOE_BM
    grid_spec = pltpu.PrefetchScalarGridSpec(
        num_scalar_prefetch=2,
        grid=(n_blocks,),
        in_specs=[
            pl.BlockSpec((MOE_BM, SUBLANES, LANES), lambda i, be, nu: (i, 0, 0)),
            pl.BlockSpec((None, 1, D_MODEL, 2 * D_EXPERT), lambda i, be, nu: (layer, be[i], 0, 0)),
            pl.BlockSpec((None, 1, 1, 2 * D_EXPERT), lambda i, be, nu: (layer, be[i], 0, 0)),
            pl.BlockSpec((None, 1, D_EXPERT, D_MODEL), lambda i, be, nu: (layer, be[i], 0, 0)),
            pl.BlockSpec((None, 1, 1, D_MODEL), lambda i, be, nu: (layer, be[i], 0, 0)),
        ],
        out_specs=pl.BlockSpec((MOE_BM, SUBLANES, LANES), lambda i, be, nu: (i, 0, 0)),
        scratch_shapes=[pltpu.VMEM((D_MODEL, 2 * D_EXPERT), BF16),
                        pltpu.VMEM((D_EXPERT, D_MODEL), BF16)],
    )
    return pl.pallas_call(
        _experts_body,
        grid_spec=grid_spec,
        out_shape=jax.ShapeDtypeStruct((n_slots, SUBLANES, LANES), F32),
        compiler_params=_cparams("arbitrary"),
        name="moe_experts",
    )(block_expert, n_used, x_sorted, w_gu, b_gu, w_down, b_down)


def _dispatch_body(slots_ref, x_ref, init_hbm, o_hbm, sem):
    del init_hbm
    tb = x_ref.shape[0]

    def issue(r, c):
        for k in range(TOP_K):
            pltpu.make_async_copy(x_ref.at[r], o_hbm.at[slots_ref[r * TOP_K + k]],
                                  sem).start(priority=k % 2)
        return c

    lax.fori_loop(0, tb, issue, 0)
    for k in range(TOP_K):
        pltpu.make_async_copy(x_ref, o_hbm.at[pl.ds(0, tb)], sem).wait()


def _dispatch(xn, slots_flat, x_sorted, *, tb):
    t = xn.shape[0]
    return pl.pallas_call(
        _dispatch_body,
        grid=(t // tb,),
        in_specs=[
            pl.BlockSpec((tb * TOP_K,), lambda i: (i,), memory_space=pltpu.SMEM),
            pl.BlockSpec((tb, SUBLANES, LANES), lambda i: (i, 0, 0)),
            pl.BlockSpec(memory_space=pl.ANY),
        ],
        out_specs=pl.BlockSpec(memory_space=pl.ANY),
        out_shape=jax.ShapeDtypeStruct(x_sorted.shape, x_sorted.dtype),
        scratch_shapes=[pltpu.SemaphoreType.DMA(())],
        input_output_aliases={2: 0},
        compiler_params=_cparams("arbitrary"),
        name="moe_dispatch",
    )(slots_flat, xn, x_sorted)


def _combine_body(slots_ref, h_ref, gate_ref, y_hbm, o_ref, buf, sem):
    tb = h_ref.shape[0]

    def row_copy(src_row, k, r):
        return pltpu.make_async_copy(y_hbm.at[src_row], buf.at[k * tb + r], sem)

    def issue(r, c):
        for k in range(TOP_K):
            row_copy(slots_ref[r * TOP_K + k], k, r).start(priority=k % 2)
        return c

    lax.fori_loop(0, tb, issue, 0)
    for k in range(TOP_K):
        pltpu.make_async_copy(y_hbm.at[pl.ds(0, tb)], buf.at[pl.ds(k * tb, tb)], sem).wait()
    rows = buf[...].reshape(TOP_K * tb, D_MODEL)
    g = gate_ref[...]
    acc = h_ref[...]
    for k in range(TOP_K):
        acc = acc + g[:, k:k + 1] * rows[k * tb:(k + 1) * tb, :]
    o_ref[...] = acc


def _combine(h, gate, slots_flat, y_slots, *, tb):
    t = h.shape[0]
    return pl.pallas_call(
        _combine_body,
        grid=(t // tb,),
        in_specs=[
            pl.BlockSpec((tb * TOP_K,), lambda i: (i,), memory_space=pltpu.SMEM),
            pl.BlockSpec((tb, D_MODEL), lambda i: (i, 0)),
            pl.BlockSpec((tb, LANES), lambda i: (i, 0)),
            pl.BlockSpec(memory_space=pl.ANY),
        ],
        out_specs=pl.BlockSpec((tb, D_MODEL), lambda i: (i, 0)),
        out_shape=jax.ShapeDtypeStruct((t, D_MODEL), F32),
        scratch_shapes=[pltpu.VMEM((TOP_K * tb, SUBLANES, LANES), F32),
                        pltpu.SemaphoreType.DMA(())],
        compiler_params=_cparams("arbitrary"),
        name="moe_combine",
    )(slots_flat, h, gate, y_slots)


def _moe(h_streams, g, wr, br, w_gu, b_gu, w_down, b_down, *, layer):
    cnt = jnp.zeros((SUBLANES, LANES), F32)
    routed = []
    for h in h_streams:
        xn, idx, gate, pos, cnt = _router(h, g, wr, br, cnt, tm=256)
        routed.append((xn, idx[:, :TOP_K], gate, pos[:, :TOP_K]))
    counts = cnt[0, :N_EXPERTS].astype(jnp.int32)
    padded = (counts + MOE_BM - 1) // MOE_BM * MOE_BM
    pad_end = jnp.cumsum(padded)
    pad_start = pad_end - padded
    t_all = sum(h.shape[0] for h in h_streams)
    n_blocks = -(-(t_all * TOP_K + N_EXPERTS * (MOE_BM - 1)) // MOE_BM)
    n_slots = n_blocks * MOE_BM
    n_used = (pad_end[-1] // MOE_BM).astype(jnp.int32)
    blk = jnp.minimum(jnp.arange(n_blocks, dtype=jnp.int32), n_used - 1) * MOE_BM
    block_expert = jnp.minimum(jnp.sum(pad_end[None, :] <= blk[:, None], axis=1),
                               N_EXPERTS - 1).astype(jnp.int32)
    slots = [(pad_start[idx] + pos).reshape(-1) for (_, idx, _, pos) in routed]
    x_sorted = jnp.zeros((n_slots, SUBLANES, LANES), F32)
    for (xn, _, _, _), s in zip(routed, slots):
        x_sorted = _dispatch(xn, s, x_sorted, tb=256)
    y_slots = _experts(x_sorted, block_expert, n_used.reshape(1), w_gu, b_gu, w_down, b_down,
                       layer=layer)
    return [_combine(h, gate, s, y_slots, tb=256)
            for h, s, (_, _, gate, _) in zip(h_streams, slots, routed)]


LOG2E = 1.0 / math.log(2.0)


def _softplus2(y):
    neg_abs = lax.bitcast_convert_type(
        lax.bitcast_convert_type(y, jnp.uint32) | jnp.uint32(0x80000000), F32)
    return jnp.maximum(y, 0.0) + jnp.log(1.0 + jnp.exp2(neg_abs)) * LOG2E


def _sb_block(y, v, r_in, suffix_mat, mask, v_transposed=False):
    sp = _softplus2(y)
    if mask is not None:
        sp = jnp.where(mask, sp, 0.0)
    s_incl = jnp.dot(sp.astype(BF16), suffix_mat, preferred_element_type=F32) + r_in
    w = jnp.exp2(y - s_incl)
    if mask is not None:
        w = jnp.where(mask, w, 0.0)
    dims = (((1,), (1 if v_transposed else 0,)), ((), ()))
    return (lax.dot_general(w.astype(BF16), v, dims, preferred_element_type=F32),
            s_incl[:, 0:1])


def _suffix_matrix(n):
    return (lax.broadcasted_iota(jnp.int32, (n, n), 0)
            >= lax.broadcasted_iota(jnp.int32, (n, n), 1)).astype(BF16)


def _attn_prompt_body(bias_ref, q_ref, k_ref, v_ref, o_ref, y_scr, sp_scr):
    hp = pl.program_id(1)
    qi = pl.program_id(2)
    tq, tk = ATTN_TQ, ATTN_TK
    q = q_ref[...]
    lane = lax.broadcasted_iota(jnp.int32, (tq, LANES), 1)
    first = lane < HEAD_DIM
    suffix = _suffix_matrix(tk)
    diag_mask = (lax.broadcasted_iota(jnp.int32, (tq, tk), 1)
                 < lax.broadcasted_iota(jnp.int32, (tq, tk), 0))
    heads = ((jnp.where(first, q, jnp.zeros_like(q)), bias_ref[2 * hp]),
             (jnp.where(first, jnp.zeros_like(q), q), bias_ref[2 * hp + 1]))

    def logits(j, masked):
        kb = k_ref[pl.ds(pl.multiple_of(j * tk, tk), tk), :]
        ys = []
        for qh, bias in heads:
            y = lax.dot_general(qh, kb, (((1,), (1,)), ((), ())),
                                preferred_element_type=F32) + bias
            if masked:
                y = jnp.where(diag_mask, y, NEG_BIG)
            ys.append(y)
        return ys

    def split_store(ys, slot):
        for h, y in enumerate(ys):
            y_scr[h, slot] = y
            sp_scr[h, slot] = _softplus2(y).astype(BF16)

    def suffix_sums(slot):
        return [jnp.dot(sp_scr[h, slot], suffix, preferred_element_type=F32) for h in range(2)]

    def finish(s2s, j, slot, carry):
        vb = v_ref[pl.ds(pl.multiple_of(j * tk, tk), tk), :]
        outs, rs = [], []
        for h, s2 in enumerate(s2s):
            s_incl = s2 + carry[h]
            w = jnp.exp2(y_scr[h, slot] - s_incl)
            outs.append(jnp.dot(w.astype(BF16), vb, preferred_element_type=F32))
            rs.append(s_incl[:, 0:1])
        return rs[0], rs[1], carry[2] + jnp.where(first, outs[0], outs[1])

    def step(j_next, j_cur, cur_slot, carry):
        ys = logits(j_next, False)
        s2s = suffix_sums(cur_slot)
        split_store(ys, 1 - cur_slot)
        return finish(s2s, j_cur, cur_slot, carry)

    def steps(j, n, carry):
        for i in range(n):
            carry = step(j - i - 1, j - i, i % 2, carry)
        return carry

    unroll = ATTN_UNROLL
    split_store(logits(qi, True), 0)
    carry = (jnp.zeros((tq, 1), F32), jnp.zeros((tq, 1), F32), jnp.zeros((tq, LANES), F32))
    carry = lax.fori_loop(0, qi // unroll, lambda u, c: steps(qi - unroll * u, unroll, c), carry)
    rest = qi % unroll
    carry = lax.fori_loop(0, rest // 2, lambda u, c: steps(rest - 2 * u, 2, c), carry)
    odd = qi % 2

    def tail(c):
        c = step(0, 1, 0, c)
        return finish(suffix_sums(1), 0, 1, c)

    def last(c):
        return finish(suffix_sums(0), 0, 0, c)

    carry = lax.cond(odd == 1, tail, last, carry)
    o_ref[...] = carry[2].astype(o_ref.dtype)


def _attn_prompt(q, k, v, bias, *, n_seq):
    t = q.shape[0]
    l = t // n_seq
    assert ATTN_TQ == ATTN_TK and l % ATTN_TQ == 0
    nq = l // ATTN_TQ
    grid_spec = pltpu.PrefetchScalarGridSpec(
        num_scalar_prefetch=0,
        grid=(n_seq, N_HEADS // 2, nq),
        in_specs=[
            pl.BlockSpec(memory_space=pltpu.SMEM),
            pl.BlockSpec((ATTN_TQ, LANES), lambda b, hp, qi: (b * nq + qi, hp)),
            pl.BlockSpec((l, LANES), lambda b, hp, qi: (b, hp)),
            pl.BlockSpec((l, LANES), lambda b, hp, qi: (b, hp)),
        ],
        out_specs=pl.BlockSpec((ATTN_TQ, LANES), lambda b, hp, qi: (b * nq + qi, hp)),
        scratch_shapes=[pltpu.VMEM((2, 2, ATTN_TQ, ATTN_TK), F32),
                        pltpu.VMEM((2, 2, ATTN_TQ, ATTN_TK), BF16)],
    )
    return pl.pallas_call(
        _attn_prompt_body,
        grid_spec=grid_spec,
        out_shape=jax.ShapeDtypeStruct((t, N_HEADS * HEAD_DIM), BF16),
        compiler_params=_cparams("parallel", "parallel", "arbitrary"),
        name="sb_attn_prompt",
    )(bias, q, k, v)


def _attn_sample_body(pt_ref, qbd_ref, kn_ref, vn_ref, bias_ref, *refs):
    pages = PAGES_PER_STEP
    k_refs = refs[:pages]
    v_refs = refs[pages:2 * pages]
    o_ref = refs[2 * pages]
    acc_ref, r_ref = refs[2 * pages + 1:]
    s = pl.program_id(1)
    n_rows = qbd_ref.shape[1]
    n_new = n_rows // N_HEADS
    qbd = qbd_ref[0]
    bias = bias_ref[...]
    suffix = _suffix_matrix(PAGE_SIZE)

    def logits(kb):
        return lax.dot_general(qbd, kb, (((1,), (1,)), ((), ())), preferred_element_type=F32) + bias

    @pl.when(s == 0)
    def _():
        col = lax.broadcasted_iota(jnp.int32, (n_rows, PAGE_SIZE), 1)
        t_of_row = lax.broadcasted_iota(jnp.int32, (n_rows, PAGE_SIZE), 0) % n_new
        contrib, r = _sb_block(logits(kn_ref[0]), vn_ref[0], jnp.zeros((n_rows, 1), F32),
                               suffix, col < t_of_row)
        acc_ref[...] = contrib
        r_ref[...] = jnp.broadcast_to(r, r_ref.shape)

    def page(ref):
        return ref[0].reshape(N_HEADS * HEAD_DIM, PAGE_SIZE).astype(BF16)

    ys = [jnp.dot(qbd, page(k_refs[i]), preferred_element_type=F32) + bias for i in range(pages)]
    sps = [_softplus2(y).astype(BF16) for y in ys]
    sums = [jnp.dot(sp, suffix, preferred_element_type=F32) for sp in sps]
    r = r_ref[:, 0:1]
    ws = []
    for y, page_sums in zip(ys, sums):
        s_incl = page_sums + r
        ws.append(jnp.exp2(y - s_incl).astype(BF16))
        r = s_incl[:, 0:1]
    contrib = None
    for i, w in enumerate(ws):
        c = lax.dot_general(w, page(v_refs[i]), (((1,), (1,)), ((), ())),
                            preferred_element_type=F32)
        contrib = c if contrib is None else contrib + c
    acc_ref[...] += contrib
    r_ref[...] = jnp.broadcast_to(r, r_ref.shape)

    @pl.when(s == pl.num_programs(1) - 1)
    def _():
        lane_head = lax.broadcasted_iota(jnp.int32, (n_new, N_HEADS * HEAD_DIM), 1) // HEAD_DIM
        out = jnp.zeros((n_new, N_HEADS * HEAD_DIM), F32)
        for h in range(N_HEADS):
            out = jnp.where(lane_head == h, acc_ref[h * n_new:(h + 1) * n_new, :], out)
        o_ref[0] = out.astype(o_ref.dtype)


def _attn_sample(q, k_new, v_new, bias, cache_k, cache_v, page_table):
    b, n_new, d = q.shape
    n_pages = page_table.shape[1]
    pages = PAGES_PER_STEP
    assert n_pages % pages == 0 and n_new <= PAGE_SIZE
    n_rows = N_HEADS * n_new
    q4 = q.reshape(b, n_new, N_HEADS, HEAD_DIM).transpose(0, 2, 1, 3)
    eye = jnp.eye(N_HEADS, dtype=q.dtype)
    qbd = (q4[:, :, :, None, :] * eye[None, :, None, :, None]).reshape(b, n_rows, d)
    pad = lambda x: jnp.pad(x, ((0, 0), (0, PAGE_SIZE - n_new), (0, 0)))
    bias_rows = jnp.broadcast_to(jnp.repeat(bias, n_new)[:, None], (n_rows, PAGE_SIZE)).astype(F32)

    def page_spec(i):
        return pl.BlockSpec((1, N_HEADS, HEAD_DIM, PAGE_SIZE),
                            lambda bb, s, pt: (pt[bb, n_pages - 1 - (s * pages + i)], 0, 0, 0))

    grid_spec = pltpu.PrefetchScalarGridSpec(
        num_scalar_prefetch=1,
        grid=(b, n_pages // pages),
        in_specs=[
            pl.BlockSpec((1, n_rows, d), lambda bb, s, pt: (bb, 0, 0)),
            pl.BlockSpec((1, PAGE_SIZE, d), lambda bb, s, pt: (bb, 0, 0)),
            pl.BlockSpec((1, PAGE_SIZE, d), lambda bb, s, pt: (bb, 0, 0)),
            pl.BlockSpec((n_rows, PAGE_SIZE), lambda bb, s, pt: (0, 0)),
        ] + [page_spec(i) for i in range(pages)] + [page_spec(i) for i in range(pages)],
        out_specs=pl.BlockSpec((1, n_new, d), lambda bb, s, pt: (bb, 0, 0)),
        scratch_shapes=[pltpu.VMEM((n_rows, d), F32), pltpu.VMEM((n_rows, LANES), F32)],
    )
    return pl.pallas_call(
        _attn_sample_body,
        grid_spec=grid_spec,
        out_shape=jax.ShapeDtypeStruct((b, n_new, d), BF16),
        compiler_params=_cparams("parallel", "arbitrary"),
        name="sb_attn_sample",
    )(page_table, qbd, pad(k_new), pad(v_new), bias_rows,
      *([cache_k] * pages), *([cache_v] * pages))


def _rmsnorm_body(x_ref, g_ref, o_ref):
    x = x_ref[...]
    ms = jnp.mean(x * x, axis=-1, keepdims=True)
    o_ref[...] = x * lax.rsqrt(ms + RMS_EPS) * g_ref[...]


def _rmsnorm(x, g, *, tm):
    t, d = x.shape
    return pl.pallas_call(
        _rmsnorm_body,
        grid=(t // tm,),
        in_specs=[pl.BlockSpec((tm, d), lambda i: (i, 0)), pl.BlockSpec((1, d), lambda i: (0, 0))],
        out_specs=pl.BlockSpec((tm, d), lambda i: (i, 0)),
        out_shape=jax.ShapeDtypeStruct((t, d), F32),
        compiler_params=_cparams("parallel"),
        name="final_rmsnorm",
    )(x, g)


def _pad_lanes(x, n=LANES, value=0.0):
    return jnp.pad(x, [(0, 0)] * (x.ndim - 1) + [(0, n - x.shape[-1])], constant_values=value)


def kernel(x_prompt, x_sample, state_conv, state_ssm, cache_k, cache_v, page_table, g_mix, w_in, conv_w, conv_b, dt_bias, a_log, d_skip, g_ssm_norm, w_ssm_out, g_kv, w_k, w_v, g_attn, w_q, w_o, sb_bias, g_ffn, w_router, b_router, w_gu, b_gu, w_down, b_down, g_final):
    bp, lp, d = x_prompt.shape
    bs, ls, _ = x_sample.shape
    tp, ts = bp * lp, bs * ls
    streams = [x_prompt.reshape(tp, d), x_sample.reshape(ts, d)]
    tms = [1024, ts]

    w_in0 = w_in[0] * g_mix[0][:, None]
    w_zx = w_in0[:, :D_INNER + CONV_DIM].astype(BF16)
    w_dt = _pad_lanes(w_in0[:, D_INNER + CONV_DIM:])
    w_out = w_ssm_out[0].astype(BF16)
    w_q1 = (w_q[0] * g_attn[0][:, None] * (HEAD_DIM ** -0.5 * LOG2E)).astype(BF16)
    bias2 = sb_bias[0] * LOG2E
    w_k1 = (w_k * g_kv[:, None]).astype(BF16)
    w_v1 = (w_v * g_kv[:, None]).astype(BF16)
    w_o1 = w_o[0].astype(BF16)
    b_gu4 = b_gu[:, :, None, :]
    b_down4 = b_down[:, :, None, :]
    wr = _pad_lanes(w_router)
    br = _pad_lanes(b_router, value=NEG_BIG)[:, None, :]
    cw = jnp.pad(conv_w[0], ((0, SUBLANES - CONV_W), (0, 0)))
    expand_mat = (jnp.arange(D_INNER)[None, :] // SSM_HEAD_DIM
                  == jnp.arange(LANES)[:, None]).astype(BF16)
    dskip_x = jnp.repeat(d_skip[0], SSM_HEAD_DIM)[None, :]
    conv_pad = lambda cs: jnp.pad(cs, ((0, 0), (SUBLANES - (CONV_W - 1), 0), (0, 0)))
    conv0 = [jnp.zeros((bp, SUBLANES, CONV_DIM), F32), conv_pad(state_conv[0])]
    ssm0 = [jnp.zeros((bp, D_INNER, D_STATE), F32), state_ssm[0].reshape(bs, D_INNER, D_STATE)]

    new_conv, new_ssm, h1 = [], [], []
    for i, (h, tm) in enumerate(zip(streams, tms)):
        n_seq, l = (bp, lp) if i == 0 else (bs, ls)
        zx, = _norm_matmul(h, w_zx, norm=True, tm=tm, tn=1024, name="in_proj")
        dt_raw, = _norm_matmul(h, w_dt, norm=True, tm=tm, tn=LANES, precision=HIGHEST,
                               name="in_proj_dt")
        if l % SSD_CHUNK:
            assert l < SSD_CHUNK and l % SUBLANES == 0
            padrows = lambda x: jnp.pad(x.reshape(n_seq, l, -1), ((0, 0), (0, SSD_CHUNK - l), (0, 0))
                                        ).reshape(n_seq * SSD_CHUNK, -1)
            zx, dt_raw = padrows(zx), padrows(dt_raw)
            valid = l
        else:
            valid = SSD_CHUNK
        y, conv_o, ssm_o = _ssd(zx, dt_raw, conv0[i], ssm0[i], cw, conv_b[0][None, :],
                                _pad_lanes(dt_bias[0])[None, :], _pad_lanes(a_log[0])[None, :],
                                dskip_x, g_ssm_norm[0][None, :], expand_mat,
                                n_seq=n_seq, valid_len=valid)
        if l % SSD_CHUNK:
            y = y.reshape(n_seq, SSD_CHUNK, D_INNER)[:, :l].reshape(n_seq * l, D_INNER)
        new_conv.append(conv_o[:, SUBLANES - (CONV_W - 1):][None])
        new_ssm.append(ssm_o.reshape(n_seq, N_SSM_HEADS, SSM_HEAD_DIM, D_STATE)[None])
        hm, = _norm_matmul(y, w_out, norm=False, res=h, tm=tm, tn=1024, name="ssm_out_proj")
        h1.append(hm)
    h1 = _moe(h1, g_ffn[0][None, :], wr[0], br[0], w_gu, b_gu4, w_down, b_down4, layer=0)

    qs, ks, vs, kbs, vbs = [], [], [], [], []
    for h, tm in zip(h1, tms):
        q, = _norm_matmul(h, w_q1, norm=True, out_dtypes=(BF16,), tm=tm, tn=1024, name="q_proj")
        k, kb = _norm_matmul(h, w_k1, norm=True, out_dtypes=(F32, BF16), tm=min(tm, 512), tn=1024,
                             split_heads=True, name="k_proj")
        v, vb = _norm_matmul(h, w_v1, norm=True, out_dtypes=(F32, BF16), tm=min(tm, 512), tn=1024,
                             split_heads=True, name="v_proj")
        qs.append(q); ks.append(k); vs.append(v); kbs.append(kb); vbs.append(vb)
    o_p = _attn_prompt(qs[0], kbs[0], vbs[0], bias2, n_seq=bp)
    n_phys = cache_k.shape[0]
    o_s = _attn_sample(qs[1].reshape(bs, ls, d), kbs[1].reshape(bs, ls, d), vbs[1].reshape(bs, ls, d),
                       bias2, cache_k.transpose(0, 2, 3, 1), cache_v.transpose(0, 2, 3, 1),
                       page_table).reshape(ts, d)
    h2 = []
    for h, o, tm in zip(h1, [o_p, o_s], tms):
        hm, = _norm_matmul(o, w_o1, norm=False, res=h, tm=tm, tn=1024, name="attn_out_proj")
        h2.append(hm)
    h2 = _moe(h2, g_ffn[1][None, :], wr[1], br[1], w_gu, b_gu4, w_down, b_down4, layer=1)

    y_p = _rmsnorm(h2[0], g_final[None, :], tm=1024).reshape(bp, lp, d)
    y_s = _rmsnorm(h2[1], g_final[None, :], tm=ts).reshape(bs, ls, d)
    kv_shape = lambda x, b, l: x.reshape(b, l, N_HEADS, HEAD_DIM)
    return (y_p, y_s, new_conv[0], new_ssm[0], kv_shape(ks[0], bp, lp), kv_shape(vs[0], bp, lp),
            new_conv[1], new_ssm[1], kv_shape(ks[1], bs, ls), kv_shape(vs[1], bs, ls))
```

```python
import functools
import math

import jax
import jax.numpy as jnp
from jax import lax
from jax.experimental import pallas as pl
from jax.experimental.pallas import tpu as pltpu

F32 = jnp.float32
BF16 = jnp.bfloat16
HIGHEST = lax.Precision.HIGHEST

LANES = 128
SUBLANES = 8
VMEM_LIMIT = 48 * 1024 * 1024

D_MODEL = 1024
D_INNER = 2048
SSM_HEAD_DIM = 64
N_SSM_HEADS = 32
D_STATE = 128
N_GROUPS = 8
GROUP_W = D_INNER // N_GROUPS
CONV_W = 4
CONV_DIM = D_INNER + 2 * N_GROUPS * D_STATE
SSD_CHUNK = 128
HEAD_DIM = 64
N_HEADS = 16
N_EXPERTS = 32
TOP_K = 4
D_EXPERT = 1024
SWIGLU_ALPHA = 1.702
SWIGLU_LIMIT = 7.0
RMS_EPS = 1e-5
PAGE_SIZE = 128

MOE_BM = 256
ATTN_TQ = 256
ATTN_TK = 256
ATTN_UNROLL = 8
PAGES_PER_STEP = 8
MOE_TB = 512
NEG_BIG = -1e30


def _cparams(*sem, flags=None):
    return pltpu.CompilerParams(dimension_semantics=sem, vmem_limit_bytes=VMEM_LIMIT, flags=flags)


def _softplus(z):
    return jnp.maximum(z, 0.0) + jnp.log1p(jnp.exp(-jnp.abs(z)))


def _split_bf16(x):
    hi = x.astype(BF16)
    lo = (x - hi.astype(F32)).astype(BF16)
    return hi, lo


def _nm_body(*refs, norm, has_res, n_out, precision):
    x_ref, w_ref = refs[0], refs[1]
    pos = 2
    res_ref = None
    if has_res:
        res_ref = refs[pos]
        pos += 1
    out_refs = refs[pos:pos + n_out]
    pos += n_out
    if norm:
        xn_ref = refs[pos]

        @pl.when(pl.program_id(1) == 0)
        def _():
            x = x_ref[...].astype(F32)
            ms = jnp.mean(x * x, axis=-1, keepdims=True)
            xn_ref[...] = (x * lax.rsqrt(ms + RMS_EPS)).astype(xn_ref.dtype)

        lhs = xn_ref[...]
    else:
        lhs = x_ref[...]
    acc = jnp.dot(lhs, w_ref[...], preferred_element_type=F32, precision=precision)
    if has_res:
        acc = acc + res_ref[...]
    for o in out_refs:
        o[...] = acc.astype(o.dtype).reshape(o.shape)


def _norm_matmul(x, w, *, norm, res=None, out_dtypes=(F32,), tm, tn, precision=None,
                 split_heads=False, name="matmul"):
    t, k = x.shape
    n = w.shape[1]
    assert t % tm == 0 and n % tn == 0
    in_specs = [pl.BlockSpec((tm, k), lambda i, j: (i, 0)),
                pl.BlockSpec((k, tn), lambda i, j: (0, j))]
    args = [x, w]
    if res is not None:
        in_specs.append(pl.BlockSpec((tm, tn), lambda i, j: (i, j)))
        args.append(res)
    out_specs = [pl.BlockSpec((tm, tn), lambda i, j: (i, j)) for _ in out_dtypes]
    out_shape = [jax.ShapeDtypeStruct((t, n), d) for d in out_dtypes]
    if split_heads:
        assert n == tn == N_HEADS * HEAD_DIM
        out_specs[0] = pl.BlockSpec((tm, N_HEADS, HEAD_DIM), lambda i, j: (i, 0, 0))
        out_shape[0] = jax.ShapeDtypeStruct((t, N_HEADS, HEAD_DIM), out_dtypes[0])
    outs = pl.pallas_call(
        functools.partial(_nm_body, norm=norm, has_res=res is not None,
                          n_out=len(out_dtypes), precision=precision),
        grid=(t // tm, n // tn),
        in_specs=in_specs,
        out_specs=out_specs,
        out_shape=out_shape,
        scratch_shapes=[pltpu.VMEM((tm, k), w.dtype)] if norm else [],
        compiler_params=_cparams("parallel", "arbitrary"),
        name=name,
    )(*args)
    return outs


def _ssd_body(zx_ref, dt_ref, conv0_ref, ssm0_ref, cw_ref, cb_ref, dtb_ref, alog_ref,
              dskip_ref, gn_ref, e_ref,
              y_ref, conv_out_ref, ssm_ref,
              ext_ref, xbc_ref, xg_ref, xd_ref, *, valid_len):
    L = SSD_CHUNK
    c = pl.program_id(1)

    @pl.when(c == 0)
    def _():
        ext_ref[0:SUBLANES, :] = conv0_ref[0]
        ssm_ref[0] = ssm0_ref[0]

    ext_ref[SUBLANES:SUBLANES + L, :] = zx_ref[:, D_INNER:D_INNER + CONV_DIM]
    strip = 512
    for s in range(CONV_DIM // strip):
        cols = slice(s * strip, (s + 1) * strip)
        acc = jnp.broadcast_to(cb_ref[:, cols], (L, strip))
        for w in range(CONV_W):
            r0 = SUBLANES - (CONV_W - 1) + w
            acc = acc + ext_ref[r0:r0 + L, cols] * cw_ref[w:w + 1, cols]
        xbc_ref[:, cols] = acc * jax.nn.sigmoid(acc)
    conv_out_ref[0] = ext_ref[valid_len:valid_len + SUBLANES, :]
    ext_ref[0:SUBLANES, :] = ext_ref[L:L + SUBLANES, :]

    lane = lax.broadcasted_iota(jnp.int32, (L, LANES), 1)
    row = lax.broadcasted_iota(jnp.int32, (L, LANES), 0)
    live = (lane < N_SSM_HEADS) & (row < valid_len)
    dt = jnp.where(live, _softplus(dt_ref[...] + dtb_ref[...]), 0.0)
    a = dt * (-jnp.exp(alog_ref[...]))
    tril = (lax.broadcasted_iota(jnp.int32, (L, L), 0)
            >= lax.broadcasted_iota(jnp.int32, (L, L), 1))
    a_cum = jnp.dot(tril.astype(F32), a, preferred_element_type=F32, precision=HIGHEST)
    a_cum_t = a_cum.T
    a_last = a_cum[L - 1:L, :]
    ea = jnp.exp(a_cum)
    w_end = dt * jnp.exp(a_last - a_cum)
    chunk_decay_t = jnp.exp(a_cum_t[:, L - 1:L])

    def expand(p):
        hi, lo = _split_bf16(p)
        e = e_ref[...]
        return (jnp.dot(hi, e, preferred_element_type=F32)
                + jnp.dot(lo, e, preferred_element_type=F32))

    dt_x = expand(dt)
    wend_x = expand(w_end)
    ea_x = expand(ea)
    xs = xbc_ref[:, 0:D_INNER]
    xg_ref[...] = (xs * dt_x).astype(BF16)
    xd_ref[...] = xs * wend_x

    for g in range(N_GROUPS):
        gcols = slice(g * GROUP_W, (g + 1) * GROUP_W)
        b_g = xbc_ref[:, D_INNER + g * D_STATE:D_INNER + (g + 1) * D_STATE].astype(BF16)
        c_g = xbc_ref[:, D_INNER + N_GROUPS * D_STATE + g * D_STATE:
                      D_INNER + N_GROUPS * D_STATE + (g + 1) * D_STATE].astype(BF16)
        cb = lax.dot_general(c_g, b_g, (((1,), (1,)), ((), ())), preferred_element_type=F32)
        s_g = ssm_ref[0, g * GROUP_W:(g + 1) * GROUP_W, :]
        y_off = lax.dot_general(c_g, s_g.astype(BF16), (((1,), (1,)), ((), ())),
                                preferred_element_type=F32) * ea_x[:, gcols]
        y_parts = []
        for hh in range(GROUP_W // SSM_HEAD_DIM):
            h = g * (GROUP_W // SSM_HEAD_DIM) + hh
            seg = a_cum[:, h:h + 1] - a_cum_t[h:h + 1, :]
            m = (cb * jnp.where(tril, jnp.exp(seg), 0.0)).astype(BF16)
            y_parts.append(jnp.dot(m, xg_ref[:, h * SSM_HEAD_DIM:(h + 1) * SSM_HEAD_DIM],
                                   preferred_element_type=F32))
        y = jnp.concatenate(y_parts, axis=1) + y_off
        y = y + xbc_ref[:, gcols] * dskip_ref[:, gcols]
        xd_t = xd_ref[:, gcols].T.astype(BF16)
        contrib = jnp.dot(xd_t, b_g, preferred_element_type=F32)
        for hh in range(GROUP_W // SSM_HEAD_DIM):
            h = g * (GROUP_W // SSM_HEAD_DIM) + hh
            rows = slice(hh * SSM_HEAD_DIM, (hh + 1) * SSM_HEAD_DIM)
            ssm_ref[0, g * GROUP_W + hh * SSM_HEAD_DIM:g * GROUP_W + (hh + 1) * SSM_HEAD_DIM, :] = (
                s_g[rows, :] * chunk_decay_t[h:h + 1, :] + contrib[rows, :])
        z = zx_ref[:, gcols]
        yz = y * (z * jax.nn.sigmoid(z))
        ms = jnp.mean(yz * yz, axis=-1, keepdims=True)
        y_ref[:, gcols] = (yz * lax.rsqrt(ms + RMS_EPS) * gn_ref[:, gcols]).astype(y_ref.dtype)


def _ssd(zx, dt_raw, conv0, ssm0, cw, cb, dtb, alog, dskip, gn, expand_mat, *, n_seq, valid_len):
    rows = zx.shape[0]
    cps = rows // n_seq // SSD_CHUNK
    width = zx.shape[1]
    const = lambda shape: pl.BlockSpec(shape, lambda b, c: (0,) * len(shape))
    return pl.pallas_call(
        functools.partial(_ssd_body, valid_len=valid_len),
        grid=(n_seq, cps),
        in_specs=[
            pl.BlockSpec((SSD_CHUNK, width), lambda b, c: (b * cps + c, 0)),
            pl.BlockSpec((SSD_CHUNK, LANES), lambda b, c: (b * cps + c, 0)),
            pl.BlockSpec((1, SUBLANES, CONV_DIM), lambda b, c: (b, 0, 0)),
            pl.BlockSpec((1, D_INNER, D_STATE), lambda b, c: (b, 0, 0)),
            const((SUBLANES, CONV_DIM)), const((1, CONV_DIM)), const((1, LANES)), const((1, LANES)),
            const((1, D_INNER)), const((1, D_INNER)), const((LANES, D_INNER)),
        ],
        out_specs=[
            pl.BlockSpec((SSD_CHUNK, D_INNER), lambda b, c: (b * cps + c, 0)),
            pl.BlockSpec((1, SUBLANES, CONV_DIM), lambda b, c: (b, 0, 0)),
            pl.BlockSpec((1, D_INNER, D_STATE), lambda b, c: (b, 0, 0)),
        ],
        out_shape=[
            jax.ShapeDtypeStruct((rows, D_INNER), BF16),
            jax.ShapeDtypeStruct((n_seq, SUBLANES, CONV_DIM), F32),
            jax.ShapeDtypeStruct((n_seq, D_INNER, D_STATE), F32),
        ],
        scratch_shapes=[
            pltpu.VMEM((SSD_CHUNK + SUBLANES, CONV_DIM), F32),
            pltpu.VMEM((SSD_CHUNK, CONV_DIM), F32),
            pltpu.VMEM((SSD_CHUNK, D_INNER), BF16),
            pltpu.VMEM((SSD_CHUNK, D_INNER), F32),
        ],
        compiler_params=_cparams("parallel", "arbitrary"),
        name="ssd_mixer",
    )(zx, dt_raw, conv0, ssm0, cw, cb, dtb, alog, dskip, gn, expand_mat)


def _router_body(x_ref, g_ref, wr_ref, br_ref, cnt0_ref,
                 xn_ref, idx_ref, gate_ref, pos_ref, cnt_ref, cnt_scr):
    tm = x_ref.shape[0]

    @pl.when(pl.program_id(0) == 0)
    def _():
        cnt_scr[...] = cnt0_ref[...]

    x = x_ref[...]
    ms = jnp.mean(x * x, axis=-1, keepdims=True)
    xh = x * lax.rsqrt(ms + RMS_EPS) * g_ref[...]
    xn_ref[...] = xh.reshape(xn_ref.shape)
    logits = jnp.dot(xh, wr_ref[...], preferred_element_type=F32, precision=HIGHEST) + br_ref[...]
    lane = lax.broadcasted_iota(jnp.int32, (tm, LANES), 1)
    lane_f = lane.astype(F32)
    work = logits
    vals, idxs, hots = [], [], []
    for _ in range(TOP_K):
        m = jnp.max(work, axis=-1, keepdims=True)
        am = jnp.min(jnp.where(work == m, lane_f, float(LANES)), axis=-1, keepdims=True)
        hot = lane_f == am
        vals.append(m)
        idxs.append(am)
        hots.append(hot)
        work = jnp.where(hot, -jnp.inf, work)
    es = [jnp.exp(v - vals[0]) for v in vals]
    den = es[0] + es[1] + es[2] + es[3]
    sel = (hots[0] | hots[1] | hots[2] | hots[3])
    sel_f = sel.astype(F32)
    before = (lax.broadcasted_iota(jnp.int32, (tm, tm), 0)
              > lax.broadcasted_iota(jnp.int32, (tm, tm), 1)).astype(BF16)
    rank = jnp.dot(before, sel_f.astype(BF16), preferred_element_type=F32) + cnt_scr[0:1, :]
    cnt_new = cnt_scr[0:1, :] + jnp.sum(sel_f, axis=0, keepdims=True)
    cnt_scr[...] = jnp.broadcast_to(cnt_new, cnt_scr.shape)
    cnt_ref[...] = jnp.broadcast_to(cnt_new, cnt_ref.shape)
    idx_o = jnp.zeros((tm, LANES), jnp.int32)
    gate_o = jnp.zeros((tm, LANES), F32)
    pos_o = jnp.zeros((tm, LANES), jnp.int32)
    for k in range(TOP_K):
        p_k = jnp.sum(jnp.where(hots[k], rank, 0.0), axis=-1, keepdims=True).astype(jnp.int32)
        idx_o = jnp.where(lane == k, idxs[k].astype(jnp.int32), idx_o)
        gate_o = jnp.where(lane == k, es[k] / den, gate_o)
        pos_o = jnp.where(lane == k, p_k, pos_o)
    idx_ref[...] = idx_o
    gate_ref[...] = gate_o
    pos_ref[...] = pos_o


def _router(x, g, wr, br, cnt0, *, tm):
    t = x.shape[0]
    row = lambda w, d: (pl.BlockSpec((tm, w), lambda i: (i, 0)), jax.ShapeDtypeStruct((t, w), d))
    tiles = (pl.BlockSpec((tm, SUBLANES, LANES), lambda i: (i, 0, 0)),
             jax.ShapeDtypeStruct((t, SUBLANES, LANES), F32))
    outs = [tiles, row(LANES, jnp.int32), row(LANES, F32), row(LANES, jnp.int32)]
    const = lambda shape: pl.BlockSpec(shape, lambda i: (0,) * len(shape))
    return pl.pallas_call(
        _router_body,
        grid=(t // tm,),
        in_specs=[pl.BlockSpec((tm, D_MODEL), lambda i: (i, 0)),
                  const((1, D_MODEL)), const((D_MODEL, LANES)), const((1, LANES)),
                  const((SUBLANES, LANES))],
        out_specs=[o[0] for o in outs] + [const((SUBLANES, LANES))],
        out_shape=[o[1] for o in outs] + [jax.ShapeDtypeStruct((SUBLANES, LANES), F32)],
        scratch_shapes=[pltpu.VMEM((SUBLANES, LANES), F32)],
        compiler_params=_cparams("arbitrary"),
        name="moe_router",
    )(x, g, wr, br, cnt0)


def _experts_body(be_ref, nused_ref, x_ref, wgu_ref, bgu_ref, wd_ref, bd_ref, o_ref,
                  wgu_bf, wd_bf):
    i = pl.program_id(0)
    live = i < nused_ref[0]

    @pl.when(live & ((i == 0) | (be_ref[i] != be_ref[jnp.maximum(i - 1, 0)])))
    def _():
        wgu_bf[...] = wgu_ref[0].astype(BF16)
        wd_bf[...] = wd_ref[0].astype(BF16)

    @pl.when(live)
    def _():
        x = x_ref[...].reshape(MOE_BM, D_MODEL).astype(BF16)
        gu = jnp.dot(x, wgu_bf[...], preferred_element_type=F32) + bgu_ref[0]
        x_glu = jnp.minimum(gu[:, :D_EXPERT], SWIGLU_LIMIT)
        x_lin = jnp.clip(gu[:, D_EXPERT:], -SWIGLU_LIMIT, SWIGLU_LIMIT)
        act = (x_lin + 1.0) * (x_glu * jax.nn.sigmoid(SWIGLU_ALPHA * x_glu))
        y = jnp.dot(act.astype(BF16), wd_bf[...], preferred_element_type=F32) + bd_ref[0]
        o_ref[...] = y.reshape(o_ref.shape)

    @pl.when(jnp.logical_not(live))
    def _():
        o_ref[...] = jnp.zeros(o_ref.shape, o_ref.dtype)


def _experts(x_sorted, block_expert, n_used, w_gu, b_gu, w_down, b_down, *, layer):
    n_slots = x_sorted.shape[0]
    n_blocks = n_slots // MOE_BM
    grid_spec = pltpu.PrefetchScalarGridSpec(
        num_scalar_prefetch=2,
        grid=(n_blocks,),
        in_specs=[
            pl.BlockSpec((MOE_BM, SUBLANES, LANES), lambda i, be, nu: (i, 0, 0)),
            pl.BlockSpec((None, 1, D_MODEL, 2 * D_EXPERT), lambda i, be, nu: (layer, be[i], 0, 0)),
            pl.BlockSpec((None, 1, 1, 2 * D_EXPERT), lambda i, be, nu: (layer, be[i], 0, 0)),
            pl.BlockSpec((None, 1, D_EXPERT, D_MODEL), lambda i, be, nu: (layer, be[i], 0, 0)),
            pl.BlockSpec((None, 1, 1, D_MODEL), lambda i, be, nu: (layer, be[i], 0, 0)),
        ],
        out_specs=pl.BlockSpec((MOE_BM, SUBLANES, LANES), lambda i, be, nu: (i, 0, 0)),
        scratch_shapes=[pltpu.VMEM((D_MODEL, 2 * D_EXPERT), BF16),
                        pltpu.VMEM((D_EXPERT, D_MODEL), BF16)],
    )
    return pl.pallas_call(
        _experts_body,
        grid_spec=grid_spec,
        out_shape=jax.ShapeDtypeStruct((n_slots, SUBLANES, LANES), F32),
        compiler_params=_cparams("arbitrary"),
        name="moe_experts",
    )(block_expert, n_used, x_sorted, w_gu, b_gu, w_down, b_down)


def _dispatch_body(slots_ref, x_ref, init_hbm, o_hbm, sem):
    del init_hbm
    tb = x_ref.shape[0]

    def issue(r, c):
        for k in range(TOP_K):
            pltpu.make_async_copy(x_ref.at[r], o_hbm.at[slots_ref[r * TOP_K + k]],
                                  sem).start(priority=k % 2)
        return c

    lax.fori_loop(0, tb, issue, 0)
    for k in range(TOP_K):
        pltpu.make_async_copy(x_ref, o_hbm.at[pl.ds(0, tb)], sem).wait()


def _dispatch(xn, slots_flat, x_sorted, *, tb):
    t = xn.shape[0]
    return pl.pallas_call(
        _dispatch_body,
        grid=(t // tb,),
        in_specs=[
            pl.BlockSpec((tb * TOP_K,), lambda i: (i,), memory_space=pltpu.SMEM),
            pl.BlockSpec((tb, SUBLANES, LANES), lambda i: (i, 0, 0)),
            pl.BlockSpec(memory_space=pl.ANY),
        ],
        out_specs=pl.BlockSpec(memory_space=pl.ANY),
        out_shape=jax.ShapeDtypeStruct(x_sorted.shape, x_sorted.dtype),
        scratch_shapes=[pltpu.SemaphoreType.DMA(())],
        input_output_aliases={2: 0},
        compiler_params=_cparams("arbitrary"),
        name="moe_dispatch",
    )(slots_flat, xn, x_sorted)


def _combine_body(slots_ref, h_ref, gate_ref, y_hbm, o_ref, buf, sem):
    tb = h_ref.shape[0]

    def row_copy(src_row, k, r):
        return pltpu.make_async_copy(y_hbm.at[src_row], buf.at[k * tb + r], sem)

    def issue(r, c):
        for k in range(TOP_K):
            row_copy(slots_ref[r * TOP_K + k], k, r).start(priority=k % 2)
        return c

    lax.fori_loop(0, tb, issue, 0)
    for k in range(TOP_K):
        pltpu.make_async_copy(y_hbm.at[pl.ds(0, tb)], buf.at[pl.ds(k * tb, tb)], sem).wait()
    rows = buf[...].reshape(TOP_K * tb, D_MODEL)
    g = gate_ref[...]
    acc = h_ref[...]
    for k in range(TOP_K):
        acc = acc + g[:, k:k + 1] * rows[k * tb:(k + 1) * tb, :]
    o_ref[...] = acc


def _combine(h, gate, slots_flat, y_slots, *, tb):
    t = h.shape[0]
    return pl.pallas_call(
        _combine_body,
        grid=(t // tb,),
        in_specs=[
            pl.BlockSpec((tb * TOP_K,), lambda i: (i,), memory_space=pltpu.SMEM),
            pl.BlockSpec((tb, D_MODEL), lambda i: (i, 0)),
            pl.BlockSpec((tb, LANES), lambda i: (i, 0)),
            pl.BlockSpec(memory_space=pl.ANY),
        ],
        out_specs=pl.BlockSpec((tb, D_MODEL), lambda i: (i, 0)),
        out_shape=jax.ShapeDtypeStruct((t, D_MODEL), F32),
        scratch_shapes=[pltpu.VMEM((TOP_K * tb, SUBLANES, LANES), F32),
                        pltpu.SemaphoreType.DMA(())],
        compiler_params=_cparams("arbitrary"),
        name="moe_combine",
    )(slots_flat, h, gate, y_slots)


def _moe(h_streams, g, wr, br, w_gu, b_gu, w_down, b_down, *, layer):
    cnt = jnp.zeros((SUBLANES, LANES), F32)
    routed = []
    for h in h_streams:
        xn, idx, gate, pos, cnt = _router(h, g, wr, br, cnt, tm=256)
        routed.append((xn, idx[:, :TOP_K], gate, pos[:, :TOP_K]))
    counts = cnt[0, :N_EXPERTS].astype(jnp.int32)
    padded = (counts + MOE_BM - 1) // MOE_BM * MOE_BM
    pad_end = jnp.cumsum(padded)
    pad_start = pad_end - padded
    t_all = sum(h.shape[0] for h in h_streams)
    n_blocks = -(-(t_all * TOP_K + N_EXPERTS * (MOE_BM - 1)) // MOE_BM)
    n_slots = n_blocks * MOE_BM
    n_used = (pad_end[-1] // MOE_BM).astype(jnp.int32)
    blk = jnp.minimum(jnp.arange(n_blocks, dtype=jnp.int32), n_used - 1) * MOE_BM
    block_expert = jnp.minimum(jnp.sum(pad_end[None, :] <= blk[:, None], axis=1),
                               N_EXPERTS - 1).astype(jnp.int32)
    slots = [(pad_start[idx] + pos).reshape(-1) for (_, idx, _, pos) in routed]
    x_sorted = jnp.zeros((n_slots, SUBLANES, LANES), F32)
    for (xn, _, _, _), s in zip(routed, slots):
        x_sorted = _dispatch(xn, s, x_sorted, tb=min(MOE_TB, xn.shape[0]))
    y_slots = _experts(x_sorted, block_expert, n_used.reshape(1), w_gu, b_gu, w_down, b_down,
                       layer=layer)
    return [_combine(h, gate, s, y_slots, tb=min(MOE_TB, h.shape[0]))
            for h, s, (_, _, gate, _) in zip(h_streams, slots, routed)]


LOG2E = 1.0 / math.log(2.0)


def _softplus2(y):
    neg_abs = lax.bitcast_convert_type(
        lax.bitcast_convert_type(y, jnp.uint32) | jnp.uint32(0x80000000), F32)
    return jnp.maximum(y, 0.0) + jnp.log(1.0 + jnp.exp2(neg_abs)) * LOG2E


def _sb_block(y, v, r_in, suffix_mat, mask, v_transposed=False):
    sp = _softplus2(y)
    if mask is not None:
        sp = jnp.where(mask, sp, 0.0)
    s_incl = jnp.dot(sp.astype(BF16), suffix_mat, preferred_element_type=F32) + r_in
    w = jnp.exp2(y - s_incl)
    if mask is not None:
        w = jnp.where(mask, w, 0.0)
    dims = (((1,), (1 if v_transposed else 0,)), ((), ()))
    return (lax.dot_general(w.astype(BF16), v, dims, preferred_element_type=F32),
            s_incl[:, 0:1])


def _suffix_matrix(n):
    return (lax.broadcasted_iota(jnp.int32, (n, n), 0)
            >= lax.broadcasted_iota(jnp.int32, (n, n), 1)).astype(BF16)


def _attn_prompt_body(bias_ref, q_ref, k_ref, v_ref, o_ref, y_scr, sp_scr):
    hp = pl.program_id(1)
    qi = pl.program_id(2)
    tq, tk = ATTN_TQ, ATTN_TK
    q = q_ref[...]
    lane = lax.broadcasted_iota(jnp.int32, (tq, LANES), 1)
    first = lane < HEAD_DIM
    suffix = _suffix_matrix(tk)
    diag_mask = (lax.broadcasted_iota(jnp.int32, (tq, tk), 1)
                 < lax.broadcasted_iota(jnp.int32, (tq, tk), 0))
    heads = ((jnp.where(first, q, jnp.zeros_like(q)), bias_ref[2 * hp]),
             (jnp.where(first, jnp.zeros_like(q), q), bias_ref[2 * hp + 1]))

    def logits(j, masked):
        kb = k_ref[pl.ds(pl.multiple_of(j * tk, tk), tk), :]
        ys = []
        for qh, bias in heads:
            y = lax.dot_general(qh, kb, (((1,), (1,)), ((), ())),
                                preferred_element_type=F32) + bias
            if masked:
                y = jnp.where(diag_mask, y, NEG_BIG)
            ys.append(y)
        return ys

    def split_store(ys, slot):
        for h, y in enumerate(ys):
            y_scr[h, slot] = y
            sp_scr[h, slot] = _softplus2(y).astype(BF16)

    def suffix_sums(slot):
        return [jnp.dot(sp_scr[h, slot], suffix, preferred_element_type=F32) for h in range(2)]

    def finish(s2s, j, slot, carry):
        vb = v_ref[pl.ds(pl.multiple_of(j * tk, tk), tk), :]
        outs, rs = [], []
        for h, s2 in enumerate(s2s):
            s_incl = s2 + carry[h]
            w = jnp.exp2(y_scr[h, slot] - s_incl)
            outs.append(jnp.dot(w.astype(BF16), vb, preferred_element_type=F32))
            rs.append(s_incl[:, 0:1])
        return rs[0], rs[1], carry[2] + jnp.where(first, outs[0], outs[1])

    def step(j_next, j_cur, cur_slot, carry):
        ys = logits(j_next, False)
        s2s = suffix_sums(cur_slot)
        split_store(ys, 1 - cur_slot)
        return finish(s2s, j_cur, cur_slot, carry)

    def steps(j, n, carry):
        for i in range(n):
            carry = step(j - i - 1, j - i, i % 2, carry)
        return carry

    unroll = ATTN_UNROLL
    split_store(logits(qi, True), 0)
    carry = (jnp.zeros((tq, 1), F32), jnp.zeros((tq, 1), F32), jnp.zeros((tq, LANES), F32))
    carry = lax.fori_loop(0, qi // unroll, lambda u, c: steps(qi - unroll * u, unroll, c), carry)
    rest = qi % unroll
    carry = lax.fori_loop(0, rest // 2, lambda u, c: steps(rest - 2 * u, 2, c), carry)
    odd = qi % 2

    def tail(c):
        c = step(0, 1, 0, c)
        return finish(suffix_sums(1), 0, 1, c)

    def last(c):
        return finish(suffix_sums(0), 0, 0, c)

    carry = lax.cond(odd == 1, tail, last, carry)
    o_ref[...] = carry[2].astype(o_ref.dtype)


def _attn_prompt(q, k, v, bias, *, n_seq):
    t = q.shape[0]
    l = t // n_seq
    assert ATTN_TQ == ATTN_TK and l % ATTN_TQ == 0
    nq = l // ATTN_TQ
    grid_spec = pltpu.PrefetchScalarGridSpec(
        num_scalar_prefetch=0,
        grid=(n_seq, N_HEADS // 2, nq),
        in_specs=[
            pl.BlockSpec(memory_space=pltpu.SMEM),
            pl.BlockSpec((ATTN_TQ, LANES), lambda b, hp, qi: (b * nq + qi, hp)),
            pl.BlockSpec((l, LANES), lambda b, hp, qi: (b, hp)),
            pl.BlockSpec((l, LANES), lambda b, hp, qi: (b, hp)),
        ],
        out_specs=pl.BlockSpec((ATTN_TQ, LANES), lambda b, hp, qi: (b * nq + qi, hp)),
        scratch_shapes=[pltpu.VMEM((2, 2, ATTN_TQ, ATTN_TK), F32),
                        pltpu.VMEM((2, 2, ATTN_TQ, ATTN_TK), BF16)],
    )
    return pl.pallas_call(
        _attn_prompt_body,
        grid_spec=grid_spec,
        out_shape=jax.ShapeDtypeStruct((t, N_HEADS * HEAD_DIM), BF16),
        compiler_params=_cparams("parallel", "parallel", "arbitrary"),
        name="sb_attn_prompt",
    )(bias, q, k, v)


def _attn_sample_body(pt_ref, qbd_ref, kn_ref, vn_ref, bias_ref, *refs):
    pages = PAGES_PER_STEP
    k_refs = refs[:pages]
    v_refs = refs[pages:2 * pages]
    o_ref = refs[2 * pages]
    acc_ref, r_ref = refs[2 * pages + 1:]
    s = pl.program_id(1)
    n_rows = qbd_ref.shape[1]
    n_new = n_rows // N_HEADS
    qbd = qbd_ref[0]
    bias = bias_ref[...]
    suffix = _suffix_matrix(PAGE_SIZE)

    def logits(kb):
        return lax.dot_general(qbd, kb, (((1,), (1,)), ((), ())), preferred_element_type=F32) + bias

    @pl.when(s == 0)
    def _():
        col = lax.broadcasted_iota(jnp.int32, (n_rows, PAGE_SIZE), 1)
        t_of_row = lax.broadcasted_iota(jnp.int32, (n_rows, PAGE_SIZE), 0) % n_new
        contrib, r = _sb_block(logits(kn_ref[0]), vn_ref[0], jnp.zeros((n_rows, 1), F32),
                               suffix, col < t_of_row)
        acc_ref[...] = contrib
        r_ref[...] = jnp.broadcast_to(r, r_ref.shape)

    def page(ref):
        return ref[0].reshape(N_HEADS * HEAD_DIM, PAGE_SIZE).astype(BF16)

    ys = [jnp.dot(qbd, page(k_refs[i]), preferred_element_type=F32) + bias for i in range(pages)]
    sps = [_softplus2(y).astype(BF16) for y in ys]
    sums = [jnp.dot(sp, suffix, preferred_element_type=F32) for sp in sps]
    r = r_ref[:, 0:1]
    ws = []
    for y, page_sums in zip(ys, sums):
        s_incl = page_sums + r
        ws.append(jnp.exp2(y - s_incl).astype(BF16))
        r = s_incl[:, 0:1]
    contrib = None
    for i, w in enumerate(ws):
        c = lax.dot_general(w, page(v_refs[i]), (((1,), (1,)), ((), ())),
                            preferred_element_type=F32)
        contrib = c if contrib is None else contrib + c
    acc_ref[...] += contrib
    r_ref[...] = jnp.broadcast_to(r, r_ref.shape)

    @pl.when(s == pl.num_programs(1) - 1)
    def _():
        lane_head = lax.broadcasted_iota(jnp.int32, (n_new, N_HEADS * HEAD_DIM), 1) // HEAD_DIM
        out = jnp.zeros((n_new, N_HEADS * HEAD_DIM), F32)
        for h in range(N_HEADS):
            out = jnp.where(lane_head == h, acc_ref[h * n_new:(h + 1) * n_new, :], out)
        o_ref[0] = out.astype(o_ref.dtype)


def _attn_sample(q, k_new, v_new, bias, cache_k, cache_v, page_table):
    b, n_new, d = q.shape
    n_pages = page_table.shape[1]
    pages = PAGES_PER_STEP
    assert n_pages % pages == 0 and n_new <= PAGE_SIZE
    n_rows = N_HEADS * n_new
    q4 = q.reshape(b, n_new, N_HEADS, HEAD_DIM).transpose(0, 2, 1, 3)
    eye = jnp.eye(N_HEADS, dtype=q.dtype)
    qbd = (q4[:, :, :, None, :] * eye[None, :, None, :, None]).reshape(b, n_rows, d)
    pad = lambda x: jnp.pad(x, ((0, 0), (0, PAGE_SIZE - n_new), (0, 0)))
    bias_rows = jnp.broadcast_to(jnp.repeat(bias, n_new)[:, None], (n_rows, PAGE_SIZE)).astype(F32)

    def page_spec(i):
        return pl.BlockSpec((1, N_HEADS, HEAD_DIM, PAGE_SIZE),
                            lambda bb, s, pt: (pt[bb, n_pages - 1 - (s * pages + i)], 0, 0, 0))

    grid_spec = pltpu.PrefetchScalarGridSpec(
        num_scalar_prefetch=1,
        grid=(b, n_pages // pages),
        in_specs=[
            pl.BlockSpec((1, n_rows, d), lambda bb, s, pt: (bb, 0, 0)),
            pl.BlockSpec((1, PAGE_SIZE, d), lambda bb, s, pt: (bb, 0, 0)),
            pl.BlockSpec((1, PAGE_SIZE, d), lambda bb, s, pt: (bb, 0, 0)),
            pl.BlockSpec((n_rows, PAGE_SIZE), lambda bb, s, pt: (0, 0)),
        ] + [page_spec(i) for i in range(pages)] + [page_spec(i) for i in range(pages)],
        out_specs=pl.BlockSpec((1, n_new, d), lambda bb, s, pt: (bb, 0, 0)),
        scratch_shapes=[pltpu.VMEM((n_rows, d), F32), pltpu.VMEM((n_rows, LANES), F32)],
    )
    return pl.pallas_call(
        _attn_sample_body,
        grid_spec=grid_spec,
        out_shape=jax.ShapeDtypeStruct((b, n_new, d), BF16),
        compiler_params=_cparams("parallel", "arbitrary"),
        name="sb_attn_sample",
    )(page_table, qbd, pad(k_new), pad(v_new), bias_rows,
      *([cache_k] * pages), *([cache_v] * pages))


def _rmsnorm_body(x_ref, g_ref, o_ref):
    x = x_ref[...]
    ms = jnp.mean(x * x, axis=-1, keepdims=True)
    o_ref[...] = x * lax.rsqrt(ms + RMS_EPS) * g_ref[...]


def _rmsnorm(x, g, *, tm):
    t, d = x.shape
    return pl.pallas_call(
        _rmsnorm_body,
        grid=(t // tm,),
        in_specs=[pl.BlockSpec((tm, d), lambda i: (i, 0)), pl.BlockSpec((1, d), lambda i: (0, 0))],
        out_specs=pl.BlockSpec((tm, d), lambda i: (i, 0)),
        out_shape=jax.ShapeDtypeStruct((t, d), F32),
        compiler_params=_cparams("parallel"),
        name="final_rmsnorm",
    )(x, g)


def _pad_lanes(x, n=LANES, value=0.0):
    return jnp.pad(x, [(0, 0)] * (x.ndim - 1) + [(0, n - x.shape[-1])], constant_values=value)


def kernel(x_prompt, x_sample, state_conv, state_ssm, cache_k, cache_v, page_table, g_mix, w_in, conv_w, conv_b, dt_bias, a_log, d_skip, g_ssm_norm, w_ssm_out, g_kv, w_k, w_v, g_attn, w_q, w_o, sb_bias, g_ffn, w_router, b_router, w_gu, b_gu, w_down, b_down, g_final):
    bp, lp, d = x_prompt.shape
    bs, ls, _ = x_sample.shape
    tp, ts = bp * lp, bs * ls
    streams = [x_prompt.reshape(tp, d), x_sample.reshape(ts, d)]
    tms = [1024, ts]

    w_in0 = w_in[0] * g_mix[0][:, None]
    w_zx = w_in0[:, :D_INNER + CONV_DIM].astype(BF16)
    w_dt = _pad_lanes(w_in0[:, D_INNER + CONV_DIM:])
    w_out = w_ssm_out[0].astype(BF16)
    w_q1 = (w_q[0] * g_attn[0][:, None] * (HEAD_DIM ** -0.5 * LOG2E)).astype(BF16)
    bias2 = sb_bias[0] * LOG2E
    w_k1 = (w_k * g_kv[:, None]).astype(BF16)
    w_v1 = (w_v * g_kv[:, None]).astype(BF16)
    w_o1 = w_o[0].astype(BF16)
    b_gu4 = b_gu[:, :, None, :]
    b_down4 = b_down[:, :, None, :]
    wr = _pad_lanes(w_router)
    br = _pad_lanes(b_router, value=NEG_BIG)[:, None, :]
    cw = jnp.pad(conv_w[0], ((0, SUBLANES - CONV_W), (0, 0)))
    expand_mat = (jnp.arange(D_INNER)[None, :] // SSM_HEAD_DIM
                  == jnp.arange(LANES)[:, None]).astype(BF16)
    dskip_x = jnp.repeat(d_skip[0], SSM_HEAD_DIM)[None, :]
    conv_pad = lambda cs: jnp.pad(cs, ((0, 0), (SUBLANES - (CONV_W - 1), 0), (0, 0)))
    conv0 = [jnp.zeros((bp, SUBLANES, CONV_DIM), F32), conv_pad(state_conv[0])]
    ssm0 = [jnp.zeros((bp, D_INNER, D_STATE), F32), state_ssm[0].reshape(bs, D_INNER, D_STATE)]

    new_conv, new_ssm, h1 = [], [], []
    for i, (h, tm) in enumerate(zip(streams, tms)):
        n_seq, l = (bp, lp) if i == 0 else (bs, ls)
        zx, = _norm_matmul(h, w_zx, norm=True, tm=tm, tn=1024, name="in_proj")
        dt_raw, = _norm_matmul(h, w_dt, norm=True, tm=tm, tn=LANES, precision=HIGHEST,
                               name="in_proj_dt")
        if l % SSD_CHUNK:
            assert l < SSD_CHUNK and l % SUBLANES == 0
            padrows = lambda x: jnp.pad(x.reshape(n_seq, l, -1), ((0, 0), (0, SSD_CHUNK - l), (0, 0))
                                        ).reshape(n_seq * SSD_CHUNK, -1)
            zx, dt_raw = padrows(zx), padrows(dt_raw)
            valid = l
        else:
            valid = SSD_CHUNK
        y, conv_o, ssm_o = _ssd(zx, dt_raw, conv0[i], ssm0[i], cw, conv_b[0][None, :],
                                _pad_lanes(dt_bias[0])[None, :], _pad_lanes(a_log[0])[None, :],
                                dskip_x, g_ssm_norm[0][None, :], expand_mat,
                                n_seq=n_seq, valid_len=valid)
        if l % SSD_CHUNK:
            y = y.reshape(n_seq, SSD_CHUNK, D_INNER)[:, :l].reshape(n_seq * l, D_INNER)
        new_conv.append(conv_o[:, SUBLANES - (CONV_W - 1):][None])
        new_ssm.append(ssm_o.reshape(n_seq, N_SSM_HEADS, SSM_HEAD_DIM, D_STATE)[None])
        hm, = _norm_matmul(y, w_out, norm=False, res=h, tm=tm, tn=1024, name="ssm_out_proj")
        h1.append(hm)
    h1 = _moe(h1, g_ffn[0][None, :], wr[0], br[0], w_gu, b_gu4, w_down, b_down4, layer=0)

    qs, ks, vs, kbs, vbs = [], [], [], [], []
    for h, tm in zip(h1, tms):
        q, = _norm_matmul(h, w_q1, norm=True, out_dtypes=(BF16,), tm=tm, tn=1024, name="q_proj")
        k, kb = _norm_matmul(h, w_k1, norm=True, out_dtypes=(F32, BF16), tm=min(tm, 512), tn=1024,
                             split_heads=True, name="k_proj")
        v, vb = _norm_matmul(h, w_v1, norm=True, out_dtypes=(F32, BF16), tm=min(tm, 512), tn=1024,
                             split_heads=True, name="v_proj")
        qs.append(q); ks.append(k); vs.append(v); kbs.append(kb); vbs.append(vb)
    o_p = _attn_prompt(qs[0], kbs[0], vbs[0], bias2, n_seq=bp)
    n_phys = cache_k.shape[0]
    o_s = _attn_sample(qs[1].reshape(bs, ls, d), kbs[1].reshape(bs, ls, d), vbs[1].reshape(bs, ls, d),
                       bias2, cache_k.transpose(0, 2, 3, 1), cache_v.transpose(0, 2, 3, 1),
                       page_table).reshape(ts, d)
    h2 = []
    for h, o, tm in zip(h1, [o_p, o_s], tms):
        hm, = _norm_matmul(o, w_o1, norm=False, res=h, tm=tm, tn=1024, name="attn_out_proj")
        h2.append(hm)
    h2 = _moe(h2, g_ffn[1][None, :], wr[1], br[1], w_gu, b_gu4, w_down, b_down4, layer=1)

    y_p = _rmsnorm(h2[0], g_final[None, :], tm=1024).reshape(bp, lp, d)
    y_s = _rmsnorm(h2[1], g_final[None, :], tm=ts).reshape(bs, ls, d)
    kv_shape = lambda x, b, l: x.reshape(b, l, N_HEADS, HEAD_DIM)
    return (y_p, y_s, new_conv[0], new_ssm[0], kv_shape(ks[0], bp, lp), kv_shape(vs[0], bp, lp),
            new_conv[1], new_ssm[1], kv_shape(ks[1], bs, ls), kv_shape(vs[1], bs, ls))
```
